```python
import math
import jax
import jax.numpy as jnp
from jax import lax
import numpy as np

D_MODEL = 1024
BATCH = 2
SEQ = 8192
DEPTH = 4

GRID_W = 64
CTX_LEN = 256
Q_BLOCK = 128
ROPE_THETA = 10000.0
NORM_EPS = 1e-6
N_MOD = 6

A_HEADS = 8
A_KV_HEADS = 2
A_DIM = 64
B_HEADS = 8
B_Q_RANK = 256
B_KV_RANK = 128
B_NOPE = 64
B_ROPE = 32
B_V = 64
MLA_SCALE = (B_NOPE + B_ROPE) ** -0.5
C_HEADS = 8
C_DIM = 32
C_V = 64
N_EXPERTS = 32
TOP_K = 4
D_EXPERT = D_MODEL
SWIGLU_ALPHA = 1.702
SWIGLU_LIMIT = 7.0
E_BLOCK = 128

A_WIDTH = A_HEADS * A_DIM
B_WIDTH = B_HEADS * B_V
C_WIDTH = C_HEADS * C_V
IN_SIZES = (A_HEADS * A_DIM, A_KV_HEADS * A_DIM, A_KV_HEADS * A_DIM,
            B_Q_RANK, B_KV_RANK, B_ROPE,
            C_HEADS * 2 * C_DIM, C_HEADS * 2 * C_DIM, C_HEADS * C_V,
            3 * D_MODEL)
IN_COLS = sum(IN_SIZES)
IN_SPLITS = tuple(int(s) for s in np.cumsum(IN_SIZES)[:-1])

kernel_name = 'hybrid_gqa_mla_diffattn_moe_dit_trunk'


def rms_norm(x, gain):
    xf = x.astype(jnp.float32)
    y = xf * lax.rsqrt(jnp.mean(xf * xf, axis=-1, keepdims=True) + NORM_EPS)
    return (y * gain.astype(jnp.float32)).astype(x.dtype)


def modulate(h, shift, scale):
    return h * (1.0 + scale) + shift


def adaln(cond, w_mod, b_mod):
    return jnp.split(jax.nn.silu(cond) @ w_mod + b_mod, N_MOD, axis=-1)


def axial_rope(n_lat, rot_dim):
    n_rows = n_lat // GRID_W
    row = jnp.repeat(jnp.arange(n_rows, dtype=jnp.float32), GRID_W)
    col = jnp.tile(jnp.arange(GRID_W, dtype=jnp.float32), n_rows)
    axis_pairs = rot_dim // 4
    inv_freq = ROPE_THETA ** (-jnp.arange(axis_pairs, dtype=jnp.float32) / axis_pairs)
    ang = jnp.concatenate([row[:, None] * inv_freq, col[:, None] * inv_freq], axis=-1)
    return jnp.cos(ang), jnp.sin(ang)


def apply_rope(x, cos, sin):
    shape = (1, cos.shape[0]) + (1,) * (x.ndim - 3) + (cos.shape[1],)
    cos = cos.reshape(shape)
    sin = sin.reshape(shape)
    x1, x2 = jnp.split(x.astype(jnp.float32), 2, axis=-1)
    return jnp.concatenate([x1 * cos - x2 * sin, x2 * cos + x1 * sin], axis=-1).astype(x.dtype)


def over_query_blocks(fn, *qs):
    bsz, n = qs[0].shape[:2]
    nb = n // Q_BLOCK
    blocks = tuple(jnp.moveaxis(q.reshape((bsz, nb, Q_BLOCK) + q.shape[2:]), 1, 0) for q in qs)
    out = lax.map(lambda args: fn(*args), blocks)
    return jnp.moveaxis(out, 0, 1).reshape((bsz, n) + out.shape[3:])


def apply_direct(fn, *qs):
    return fn(*qs)


def gqa_attend(q, k, v):
    b, nq, h, d = q.shape
    g = k.shape[2]
    qg = q.reshape(b, nq, g, h // g, d)
    s = jnp.einsum('bqgrd,bkgd->bgrqk', qg, k).astype(jnp.float32) * (d ** -0.5)
    p = jax.nn.softmax(s, axis=-1).astype(v.dtype)
    o = jnp.einsum('bgrqk,bkgd->bqgrd', p, v)
    return o.reshape(b, nq, h, v.shape[-1])


def mla_attend(q_nope, q_rope, k_nope, k_rope, v):
    s = (jnp.einsum('bqhd,bkhd->bhqk', q_nope, k_nope)
         + jnp.einsum('bqhd,bkd->bhqk', q_rope, k_rope)).astype(jnp.float32) * MLA_SCALE
    p = jax.nn.softmax(s, axis=-1).astype(v.dtype)
    return jnp.einsum('bhqk,bkhd->bqhd', p, v)


def diff_attend(q, k, v, lam):
    s = jnp.einsum('bqhcd,bkhcd->bhcqk', q, k).astype(jnp.float32) * (q.shape[-1] ** -0.5)
    p = jax.nn.softmax(s, axis=-1)
    w = (p[:, :, 0] - lam * p[:, :, 1]).astype(v.dtype)
    return jnp.einsum('bhqk,bkhd->bqhd', w, v)


def mixer_inputs(h, w_in, a_q_norm, a_k_norm, b_q_a_norm, b_kv_a_norm, b_w_uq, b_w_ukv,
                 b_q_norm, b_k_norm, c_q_norm, c_k_norm, rope_a, rope_b, rope_c):
    bsz, n, _ = h.shape
    (a_q, a_k, a_v, b_cq, b_ckv, b_kr, c_q, c_k, c_v, gates) = jnp.split(h @ w_in, IN_SPLITS, axis=-1)
    a_q = rms_norm(a_q.reshape(bsz, n, A_HEADS, A_DIM), a_q_norm)
    a_k = rms_norm(a_k.reshape(bsz, n, A_KV_HEADS, A_DIM), a_k_norm)
    a_v = a_v.reshape(bsz, n, A_KV_HEADS, A_DIM)
    b_q = (rms_norm(b_cq, b_q_a_norm) @ b_w_uq).reshape(bsz, n, B_HEADS, B_NOPE + B_ROPE)
    b_kv = (rms_norm(b_ckv, b_kv_a_norm) @ b_w_ukv).reshape(bsz, n, B_HEADS, B_NOPE + B_V)
    b_qn = rms_norm(b_q[..., :B_NOPE], b_q_norm[:B_NOPE])
    b_qr = rms_norm(b_q[..., B_NOPE:], b_q_norm[B_NOPE:])
    b_kn = rms_norm(b_kv[..., :B_NOPE], b_k_norm[:B_NOPE])
    b_v = b_kv[..., B_NOPE:]
    b_kr = rms_norm(b_kr, b_k_norm[B_NOPE:])
    c_q = rms_norm(c_q.reshape(bsz, n, C_HEADS, 2, C_DIM), c_q_norm)
    c_k = rms_norm(c_k.reshape(bsz, n, C_HEADS, 2, C_DIM), c_k_norm)
    c_v = c_v.reshape(bsz, n, C_HEADS, C_V)
    if rope_a is not None:
        a_q = apply_rope(a_q, *rope_a)
        a_k = apply_rope(a_k, *rope_a)
        b_qr = apply_rope(b_qr, *rope_b)
        b_kr = apply_rope(b_kr, *rope_b)
        c_q = apply_rope(c_q, *rope_c)
        c_k = apply_rope(c_k, *rope_c)
    return (a_q, b_qn, b_qr, c_q), (a_k, a_v, b_kn, b_kr, b_v, c_k, c_v), gates


def token_mixers(queries, keys_values, lam, blocked):
    a_q, b_qn, b_qr, c_q = queries
    a_k, a_v, b_kn, b_kr, b_v, c_k, c_v = keys_values
    run = over_query_blocks if blocked else apply_direct
    o_a = run(lambda q: gqa_attend(q, a_k, a_v), a_q)
    o_b = run(lambda qn, qr: mla_attend(qn, qr, b_kn, b_kr, b_v), b_qn, b_qr)
    o_c = run(lambda q: diff_attend(q, c_k, c_v, lam), c_q)
    return o_a, o_b, o_c


def merge_branches(o_a, o_b, o_c, gates, c_subln, lam_init, w_o_a, w_o_b, w_o_c, w_out):
    bsz, n = gates.shape[:2]
    o_c = rms_norm(o_c, c_subln) * (1.0 - lam_init)
    g_a, g_b, g_c = jnp.split(jax.nn.sigmoid(gates), 3, axis=-1)
    y = (g_a * (o_a.reshape(bsz, n, A_WIDTH) @ w_o_a)
         + g_b * (o_b.reshape(bsz, n, B_WIDTH) @ w_o_b)
         + g_c * (o_c.reshape(bsz, n, C_WIDTH) @ w_o_c))
    return y @ w_out


def moe_ffn(h, router_w, router_b, w_gu, b_gu, w_down, b_down):
    bsz, n, d = h.shape
    t = h.reshape(-1, d)
    n_tok = t.shape[0]
    n_assign = n_tok * TOP_K
    logits = (t @ router_w + router_b).astype(jnp.float32)
    top_val, top_idx = lax.top_k(logits, TOP_K)
    top_w = jax.nn.softmax(top_val, axis=-1)
    flat_e = top_idx.reshape(-1)
    flat_t = jnp.repeat(jnp.arange(n_tok, dtype=jnp.int32), TOP_K)
    flat_w = top_w.reshape(-1)
    order = jnp.argsort(flat_e)
    e_sorted = flat_e[order]
    counts = jnp.bincount(flat_e, length=N_EXPERTS)
    starts = jnp.cumsum(counts) - counts
    padded = (counts + E_BLOCK - 1) // E_BLOCK * E_BLOCK
    pad_ends = jnp.cumsum(padded)
    pad_starts = pad_ends - padded
    dest = pad_starts[e_sorted] + (jnp.arange(n_assign) - starts[e_sorted])
    n_blocks = (n_assign + N_EXPERTS * (E_BLOCK - 1) + E_BLOCK - 1) // E_BLOCK
    cap = n_blocks * E_BLOCK
    slot_tok = jnp.full((cap,), n_tok, jnp.int32).at[dest].set(flat_t[order])
    slot_w = jnp.zeros((cap,), jnp.float32).at[dest].set(flat_w[order])
    block_e = jnp.minimum(jnp.searchsorted(pad_ends, jnp.arange(n_blocks) * E_BLOCK, side='right'),
                          N_EXPERTS - 1)
    t_pad = jnp.concatenate([t, jnp.zeros((1, d), t.dtype)], axis=0)
    xb = t_pad[slot_tok].reshape(n_blocks, E_BLOCK, d)

    def expert_block(args):
        xblk, e = args
        gu = xblk @ w_gu[e] + b_gu[e]
        glu, lin = jnp.split(gu, 2, axis=-1)
        glu = jnp.minimum(glu, SWIGLU_LIMIT)
        lin = jnp.clip(lin, -SWIGLU_LIMIT, SWIGLU_LIMIT)
        act = glu * jax.nn.sigmoid(SWIGLU_ALPHA * glu) * (lin + 1.0)
        return act @ w_down[e] + b_down[e]

    yb = lax.map(expert_block, (xb, block_e)).reshape(cap, d)
    y = jax.ops.segment_sum(yb * slot_w[:, None].astype(yb.dtype), slot_tok, num_segments=n_tok + 1)[:n_tok]
    return y.reshape(bsz, n, d)


def setup_inputs(seed: int = 0) -> dict:
    key = jax.random.key(seed)
    ks = iter(jax.random.split(key, 40))
    f32 = jnp.float32
    L, D = DEPTH, D_MODEL

    def normal(shape, scale):
        return jax.random.normal(next(ks), shape, f32) * scale

    def gain(n):
        return 1.0 + normal((L, n), 0.02)

    return {
        'x': normal((BATCH, SEQ, D), 1.0),
        'c': normal((BATCH, D), 1.0),
        'ctx': normal((BATCH, CTX_LEN, D), 1.0),
        'c_ctx': normal((D,), 1.0),
        'w_mod': normal((L, D, N_MOD * D), 0.5 * D ** -0.5),
        'b_mod': normal((L, N_MOD * D), 0.01),
        'norm_mix': gain(D),
        'norm_ffn': gain(D),
        'w_in': normal((L, D, IN_COLS), D ** -0.5),
        'a_q_norm': gain(A_DIM),
        'a_k_norm': gain(A_DIM),
        'b_q_a_norm': gain(B_Q_RANK),
        'b_kv_a_norm': gain(B_KV_RANK),
        'b_w_uq': normal((L, B_Q_RANK, B_HEADS * (B_NOPE + B_ROPE)), B_Q_RANK ** -0.5),
        'b_w_ukv': normal((L, B_KV_RANK, B_HEADS * (B_NOPE + B_V)), B_KV_RANK ** -0.5),
        'b_q_norm': gain(B_NOPE + B_ROPE),
        'b_k_norm': gain(B_NOPE + B_ROPE),
        'c_q_norm': gain(C_DIM),
        'c_k_norm': gain(C_DIM),
        'c_lambda': normal((L, 4, C_DIM), 0.1),
        'c_subln': gain(C_V),
        'w_o_a': normal((L, A_WIDTH, D), A_WIDTH ** -0.5),
        'w_o_b': normal((L, B_WIDTH, D), B_WIDTH ** -0.5),
        'w_o_c': normal((L, C_WIDTH, D), C_WIDTH ** -0.5),
        'w_out': normal((L, D, D), D ** -0.5),
        'router_w': normal((L, D, N_EXPERTS), D ** -0.5),
        'router_b': normal((L, N_EXPERTS), 0.01),
        'exp_w_gu': normal((L, N_EXPERTS, D, 2 * D_EXPERT), D ** -0.5),
        'exp_b_gu': normal((L, N_EXPERTS, 2 * D_EXPERT), 0.01),
        'exp_w_down': normal((L, N_EXPERTS, D_EXPERT, D), D_EXPERT ** -0.5),
        'exp_b_down': normal((L, N_EXPERTS, D), 0.01),
    }


def reference(x, c, ctx, c_ctx, w_mod, b_mod, norm_mix, norm_ffn, w_in, a_q_norm, a_k_norm,
              b_q_a_norm, b_kv_a_norm, b_w_uq, b_w_ukv, b_q_norm, b_k_norm, c_q_norm, c_k_norm,
              c_lambda, c_subln, w_o_a, w_o_b, w_o_c, w_out, router_w, router_b,
              exp_w_gu, exp_b_gu, exp_w_down, exp_b_down):
    n_lat = x.shape[1]
    n_ctx = ctx.shape[1]
    rope_a = axial_rope(n_lat, A_DIM)
    rope_b = axial_rope(n_lat, B_ROPE)
    rope_c = axial_rope(n_lat, C_DIM)
    x_lat, x_ctx = x, ctx
    for l in range(DEPTH):
        last = l == DEPTH - 1
        lam_init = 0.8 - 0.6 * math.exp(-0.3 * l)
        lq1, lk1, lq2, lk2 = c_lambda[l].astype(jnp.float32)
        lam = jnp.exp(jnp.sum(lq1 * lk1)) - jnp.exp(jnp.sum(lq2 * lk2)) + lam_init
        sh1, sc1, g1, sh2, sc2, g2 = adaln(c[:, None, :], w_mod[l], b_mod[l])
        csh1, csc1, cg1, csh2, csc2, cg2 = adaln(c_ctx, w_mod[l], b_mod[l])

        def feats(h, ra, rb, rc):
            return mixer_inputs(h, w_in[l], a_q_norm[l], a_k_norm[l], b_q_a_norm[l], b_kv_a_norm[l],
                                b_w_uq[l], b_w_ukv[l], b_q_norm[l], b_k_norm[l], c_q_norm[l],
                                c_k_norm[l], ra, rb, rc)

        def merge(outs, gates):
            return merge_branches(*outs, gates, c_subln[l], lam_init, w_o_a[l], w_o_b[l], w_o_c[l], w_out[l])

        h_lat = modulate(rms_norm(x_lat, norm_mix[l]), sh1, sc1)
        h_ctx = modulate(rms_norm(x_ctx, norm_mix[l]), csh1, csc1)
        q_lat, kv_lat, gate_lat = feats(h_lat, rope_a, rope_b, rope_c)
        q_ctx, kv_ctx, gate_ctx = feats(h_ctx, None, None, None)
        kv_all = tuple(jnp.concatenate([kc, kl], axis=1) for kc, kl in zip(kv_ctx, kv_lat))
        x_lat = x_lat + g1 * merge(token_mixers(q_lat, kv_all, lam, True), gate_lat)
        if not last:
            x_ctx = x_ctx + cg1 * merge(token_mixers(q_ctx, kv_ctx, lam, False), gate_ctx)

        moe = lambda h: moe_ffn(h, router_w[l], router_b[l], exp_w_gu[l], exp_b_gu[l],
                                exp_w_down[l], exp_b_down[l])
        h_lat = modulate(rms_norm(x_lat, norm_ffn[l]), sh2, sc2)
        if last:
            x_lat = x_lat + g2 * moe(h_lat)
        else:
            h_ctx = modulate(rms_norm(x_ctx, norm_ffn[l]), csh2, csc2)
            y = moe(jnp.concatenate([h_ctx, h_lat], axis=1))
            x_ctx = x_ctx + cg2 * y[:, :n_ctx]
            x_lat = x_lat + g2 * y[:, n_ctx:]
    return x_lat
```

```python
import functools
import math

import numpy as np
import jax
import jax.numpy as jnp
from jax import lax
from jax.experimental import pallas as pl
from jax.experimental.pallas import tpu as pltpu

F32 = jnp.float32
BF16 = jnp.bfloat16

D_MODEL = 1024
GRID_W = 64
ROPE_THETA = 10000.0
NORM_EPS = 1e-6
N_MOD = 6
A_HEADS, A_KV_HEADS, A_DIM = 8, 2, 64
B_HEADS, B_Q_RANK, B_KV_RANK, B_NOPE, B_ROPE, B_V = 8, 256, 128, 64, 32, 64
MLA_SCALE = (B_NOPE + B_ROPE) ** -0.5
C_HEADS, C_DIM, C_V = 8, 32, 64
N_EXPERTS, TOP_K = 32, 4
SWIGLU_ALPHA, SWIGLU_LIMIT = 1.702, 7.0
HEAD_V = 64
ATTN_WIDTH = 512

_IN_SIZES = (A_HEADS * A_DIM, A_KV_HEADS * A_DIM, A_KV_HEADS * A_DIM, B_Q_RANK, B_KV_RANK, B_ROPE,
             C_HEADS * 2 * C_DIM, C_HEADS * 2 * C_DIM, C_HEADS * C_V, 3 * D_MODEL)
_OFF = tuple(int(v) for v in np.cumsum((0,) + _IN_SIZES))
(_O_AQ, _O_AK, _O_AV, _O_BCQ, _O_BCKV, _O_BKR, _O_CQ, _O_CK, _O_CV, _O_GATE, _O_END) = _OFF
B_QW = B_NOPE + B_ROPE

_G_SIZES = (A_DIM, A_DIM, B_Q_RANK, B_KV_RANK, B_QW, B_QW, C_DIM, C_DIM)
_GOFF = tuple(int(v) for v in np.cumsum((0,) + _G_SIZES))
(_G_AQ, _G_AK, _G_BQA, _G_BKVA, _G_BQ, _G_BK, _G_CQ, _G_CK, _G_END) = _GOFF

LANES = 128
TM = 256
MOE_TB = 256
VMEM_LIMIT = 56 * 1024 * 1024

_NT = (((1,), (1,)), ((), ()))
_TN = (((0,), (0,)), ((), ()))


def _cparams(sem):
    return pltpu.CompilerParams(dimension_semantics=sem, vmem_limit_bytes=VMEM_LIMIT)


def _adaln_kernel(cond_ref, w_ref, b_ref, o_ref):
    cnd = cond_ref[...]
    s = (cnd * jax.nn.sigmoid(cnd)).astype(BF16)
    o_ref[0] = jnp.dot(s, w_ref[0].astype(BF16), preferred_element_type=F32) + b_ref[0]


def _adaln(cond8, w_mod, b_mod):
    depth, d, nm = w_mod.shape
    tn = 1536
    return pl.pallas_call(
        _adaln_kernel,
        grid=(depth, nm // tn),
        in_specs=[pl.BlockSpec((8, d), lambda l, j: (0, 0)),
                  pl.BlockSpec((1, d, tn), lambda l, j: (l, 0, j)),
                  pl.BlockSpec((1, 1, tn), lambda l, j: (l, 0, j))],
        out_specs=pl.BlockSpec((1, 8, tn), lambda l, j: (l, 0, j)),
        out_shape=jax.ShapeDtypeStruct((depth, 8, nm), F32),
        compiler_params=_cparams(("arbitrary", "arbitrary")),
        name="adaln",
    )(cond8, w_mod, b_mod.reshape(depth, 1, nm))


def _rms_tokens(x, gain):
    ms = jnp.mean(x * x, axis=-1, keepdims=True)
    return x * lax.rsqrt(ms + NORM_EPS) * gain


def _rms_rows(x, g):
    ms = jnp.mean(x * x, axis=0, keepdims=True)
    return x * lax.rsqrt(ms + NORM_EPS) * g


def _rope_rows(y, cos, sin):
    r = y.shape[0] // 2
    y1, y2 = y[:r], y[r:]
    return y1 * cos - y2 * sin, y2 * cos + y1 * sin


def _pre_kernel(x_ref, mod_ref, gmix_ref, w_ref, wuq_ref, wukv_ref, gcol_ref, rope_ref,
                qa_ref, ka_ref, va_ref, qb_ref, kb_ref, vb_ref, qc_ref, kc_ref, vc_ref, gates_ref):
    x = x_ref[0]
    sh1 = mod_ref[0, 0, 0:1, :]
    sc1 = mod_ref[0, 0, 1:2, :]
    h = _rms_tokens(x, gmix_ref[...]) * (1.0 + sc1) + sh1
    hb = h.astype(BF16)
    proj = lax.dot_general(w_ref[0:_O_GATE, :], hb, _NT, preferred_element_type=F32)
    gates = lax.dot_general(w_ref[_O_GATE:_O_END, :], hb, _NT, preferred_element_type=F32)
    gates_ref[0, 0] = jax.nn.sigmoid(gates)

    cos64, sin64 = rope_ref[0, 0:32, :], rope_ref[0, 32:64, :]
    cos32, sin32 = rope_ref[0, 64:80, :], rope_ref[0, 80:96, :]

    def gcol(a, b):
        return gcol_ref[a:b, :]

    g_aq, g_ak = gcol(_G_AQ, _G_AK), gcol(_G_AK, _G_BQA)
    a_scale = A_DIM ** -0.5
    for hd in range(A_HEADS):
        r0 = _O_AQ + hd * A_DIM
        o1, o2 = _rope_rows(_rms_rows(proj[r0:r0 + A_DIM], g_aq), cos64, sin64)
        qa_ref[0, 0, hd * A_DIM:hd * A_DIM + 32, :] = (o1 * a_scale).astype(BF16)
        qa_ref[0, 0, hd * A_DIM + 32:(hd + 1) * A_DIM, :] = (o2 * a_scale).astype(BF16)
    k_parts = []
    for g in range(A_KV_HEADS):
        r0 = _O_AK + g * A_DIM
        o1, o2 = _rope_rows(_rms_rows(proj[r0:r0 + A_DIM], g_ak), cos64, sin64)
        k_parts += [o1, o2]
    ka_ref[0] = jnp.concatenate(k_parts, axis=0).T.astype(BF16)
    va_ref[0, 0] = proj[_O_AV:_O_BCQ].astype(BF16)

    g_bq, g_bk = gcol(_G_BQ, _G_BK), gcol(_G_BK, _G_CQ)
    cqn = _rms_rows(proj[_O_BCQ:_O_BCKV], gcol(_G_BQA, _G_BKVA)).astype(BF16)
    bq = jnp.dot(wuq_ref[...], cqn, preferred_element_type=F32)
    for hd in range(B_HEADS):
        r0 = hd * B_QW
        qn = _rms_rows(bq[r0:r0 + B_NOPE], g_bq[:B_NOPE])
        qr = _rms_rows(bq[r0 + B_NOPE:r0 + B_QW], g_bq[B_NOPE:])
        r1, r2 = _rope_rows(qr, cos32, sin32)
        qb_ref[0, 0, r0:r0 + B_NOPE, :] = (qn * MLA_SCALE).astype(BF16)
        qb_ref[0, 0, r0 + B_NOPE:r0 + B_NOPE + 16, :] = (r1 * MLA_SCALE).astype(BF16)
        qb_ref[0, 0, r0 + B_NOPE + 16:r0 + B_QW, :] = (r2 * MLA_SCALE).astype(BF16)
    ckvn = _rms_rows(proj[_O_BCKV:_O_BKR], gcol(_G_BKVA, _G_BQ)).astype(BF16)
    bkv = jnp.dot(wukv_ref[...], ckvn, preferred_element_type=F32)
    kr = _rms_rows(proj[_O_BKR:_O_CQ], g_bk[B_NOPE:])
    kr1, kr2 = _rope_rows(kr, cos32, sin32)
    zpad = jnp.zeros((LANES - B_QW, kr1.shape[1]), F32)
    k_parts = []
    for hd in range(B_HEADS):
        r0 = hd * (B_NOPE + B_V)
        kn = _rms_rows(bkv[r0:r0 + B_NOPE], g_bk[:B_NOPE])
        k_parts += [kn, kr1, kr2, zpad]
        vb_ref[0, 0, hd * B_V:(hd + 1) * B_V, :] = bkv[r0 + B_NOPE:r0 + B_NOPE + B_V].astype(BF16)
    kb_ref[0] = jnp.concatenate(k_parts, axis=0).T.astype(BF16)

    g_cq, g_ck = gcol(_G_CQ, _G_CK), gcol(_G_CK, _G_END)
    c_scale = C_DIM ** -0.5
    k_parts = []
    for j in range(2 * C_HEADS):
        r0 = _O_CQ + j * C_DIM
        o1, o2 = _rope_rows(_rms_rows(proj[r0:r0 + C_DIM], g_cq), cos32, sin32)
        qc_ref[0, 0, j * C_DIM:j * C_DIM + 16, :] = (o1 * c_scale).astype(BF16)
        qc_ref[0, 0, j * C_DIM + 16:(j + 1) * C_DIM, :] = (o2 * c_scale).astype(BF16)
        r0 = _O_CK + j * C_DIM
        o1, o2 = _rope_rows(_rms_rows(proj[r0:r0 + C_DIM], g_ck), cos32, sin32)
        k_parts += [o1, o2]
    kc_ref[0] = jnp.concatenate(k_parts, axis=0).T.astype(BF16)
    vc_ref[0, 0] = proj[_O_CV:_O_GATE].astype(BF16)


def _pre_mixer(xs, modtab, gmix, w_int, wuq_t, wukv_t, gcol, rope, n_ctx_tiles):
    bsz, ntok, d = xs.shape
    nt = ntok // TM

    def fm(rows, dtype=BF16):
        return (jax.ShapeDtypeStruct((bsz, nt, rows, TM), dtype),
                pl.BlockSpec((1, 1, rows, TM), lambda b, i: (b, i, 0, 0)))

    def tmaj(cols):
        return (jax.ShapeDtypeStruct((bsz, ntok, cols), BF16),
                pl.BlockSpec((1, TM, cols), lambda b, i: (b, i, 0)))

    outs = [fm(A_HEADS * A_DIM), tmaj(LANES), fm(A_KV_HEADS * A_DIM),
            fm(B_HEADS * B_QW), tmaj(B_HEADS * LANES), fm(B_HEADS * B_V),
            fm(2 * C_HEADS * C_DIM), tmaj(2 * C_HEADS * C_DIM), fm(C_HEADS * C_V),
            fm(3 * D_MODEL, F32)]
    const = lambda b, i: (0, 0)
    return pl.pallas_call(
        _pre_kernel,
        grid=(bsz, nt),
        in_specs=[pl.BlockSpec((1, TM, d), lambda b, i: (b, i, 0)),
                  pl.BlockSpec((1, 1, N_MOD, d), lambda b, i: (b, jnp.where(i < n_ctx_tiles, 0, 1), 0, 0)),
                  pl.BlockSpec((1, d), const),
                  pl.BlockSpec(w_int.shape, const),
                  pl.BlockSpec(wuq_t.shape, const),
                  pl.BlockSpec(wukv_t.shape, const),
                  pl.BlockSpec(gcol.shape, const),
                  pl.BlockSpec((1, rope.shape[1], TM), lambda b, i: (i, 0, 0))],
        out_specs=[o[1] for o in outs],
        out_shape=[o[0] for o in outs],
        compiler_params=_cparams(("arbitrary", "arbitrary")),
        name="pre_mixer",
    )(xs, modtab, gmix, w_int, wuq_t, wukv_t, gcol, rope)


def _attn_kernel(lam_ref, q_ref, k_ref, v_ref, gsub_ref, o_ref, *, units, heads, diff, out_scale,
                 n_ctx_tiles, n_tiles):
    i = pl.program_id(1)
    n_kv = jnp.where(i < n_ctx_tiles, n_ctx_tiles, n_tiles)
    tq = q_ref.shape[3]

    def run_unit(q_row0, q_rows, pad_off, k_blk, v_row0):
        q = q_ref[0, 0, q_row0:q_row0 + q_rows, :]
        parts = []
        if pad_off:
            parts.append(jnp.zeros((pad_off, tq), BF16))
        parts.append(q)
        rest = LANES - pad_off - q_rows
        if rest:
            parts.append(jnp.zeros((rest, tq), BF16))
        qpad = jnp.concatenate(parts, axis=0) if len(parts) > 1 else q

        def body(kv, carry):
            m, l, acc = carry
            row0 = pl.multiple_of(kv * TM, TM)
            k = k_ref[0, pl.ds(row0, TM), k_blk * LANES:(k_blk + 1) * LANES]
            s = jnp.dot(k, qpad, preferred_element_type=F32)
            m_new = jnp.maximum(m, jnp.max(s, axis=0, keepdims=True))
            alpha = jnp.exp(m - m_new)
            p = jnp.exp(s - m_new)
            l = alpha * l + jnp.sum(p, axis=0, keepdims=True)
            v = v_ref[0, kv, v_row0:v_row0 + HEAD_V, :]
            acc = alpha * acc + jnp.dot(v, p.astype(BF16), preferred_element_type=F32)
            return m_new, l, acc

        init = (jnp.full((1, tq), -1e30, F32), jnp.zeros((1, tq), F32), jnp.zeros((HEAD_V, tq), F32))
        _, l, acc = lax.fori_loop(0, n_kv, body, init)
        return acc / l

    for hd in range(heads):
        if diff:
            o1 = run_unit(*units[2 * hd])
            o2 = run_unit(*units[2 * hd + 1])
            o = o1 - lam_ref[0] * o2
            o = _rms_rows(o, gsub_ref[...]) * out_scale
        else:
            o = run_unit(*units[hd])
        o_ref[0, 0, hd * HEAD_V:(hd + 1) * HEAD_V, :] = o.astype(BF16)


def _attention(lam, qt, k, vt, gsub, *, units, diff, out_scale, n_ctx_tiles, name):
    bsz, nt, qrows, _ = qt.shape
    ntok, kcols = k.shape[1], k.shape[2]
    vrows = vt.shape[2]
    heads = ATTN_WIDTH // HEAD_V
    kern = functools.partial(_attn_kernel, units=units, heads=heads, diff=diff, out_scale=out_scale,
                             n_ctx_tiles=n_ctx_tiles, n_tiles=nt)
    return pl.pallas_call(
        kern,
        grid=(bsz, nt),
        in_specs=[pl.BlockSpec(memory_space=pltpu.SMEM),
                  pl.BlockSpec((1, 1, qrows, TM), lambda b, i: (b, i, 0, 0)),
                  pl.BlockSpec((1, ntok, kcols), lambda b, i: (b, 0, 0), pipeline_mode=pl.Buffered(1)),
                  pl.BlockSpec((1, nt, vrows, TM), lambda b, i: (b, 0, 0, 0), pipeline_mode=pl.Buffered(1)),
                  pl.BlockSpec(gsub.shape, lambda b, i: (0, 0))],
        out_specs=pl.BlockSpec((1, 1, ATTN_WIDTH, TM), lambda b, i: (b, i, 0, 0)),
        out_shape=jax.ShapeDtypeStruct((bsz, nt, ATTN_WIDTH, TM), BF16),
        compiler_params=_cparams(("arbitrary", "arbitrary")),
        name=name,
    )(lam, qt, k, vt, gsub)


_UNITS_A = tuple((h * A_DIM, A_DIM, A_DIM * (h // (A_HEADS // A_KV_HEADS)), 0,
                  HEAD_V * (h // (A_HEADS // A_KV_HEADS))) for h in range(A_HEADS))
_UNITS_B = tuple((h * B_QW, B_QW, 0, h, h * B_V) for h in range(B_HEADS))
_UNITS_C = tuple((h * 2 * C_DIM + c * C_DIM, C_DIM, C_DIM * (2 * (h % 2) + c), h // 2, h * C_V)
                 for h in range(C_HEADS) for c in range(2))


def _merge_kernel(x_ref, mod_ref, oa_ref, ob_ref, oc_ref, gates_ref, woa_ref, wob_ref, woc_ref, wout_ref,
                  gffn_ref, rw_ref, rb_ref, xo_ref, h2_ref, tidx_ref, tw_ref):
    d = D_MODEL
    ya = jnp.dot(woa_ref[...], oa_ref[0, 0], preferred_element_type=F32)
    yb = jnp.dot(wob_ref[...], ob_ref[0, 0], preferred_element_type=F32)
    yc = jnp.dot(woc_ref[...], oc_ref[0, 0], preferred_element_type=F32)
    yt = (gates_ref[0, 0, 0:d, :] * ya + gates_ref[0, 0, d:2 * d, :] * yb
          + gates_ref[0, 0, 2 * d:3 * d, :] * yc)
    out = lax.dot_general(yt.astype(BF16), wout_ref[...], _TN, preferred_element_type=F32)
    g1 = mod_ref[0, 0, 2:3, :]
    x = x_ref[0] + g1 * out
    xo_ref[0] = x
    sh2, sc2 = mod_ref[0, 0, 3:4, :], mod_ref[0, 0, 4:5, :]
    h2 = _rms_tokens(x, gffn_ref[...]) * (1.0 + sc2) + sh2
    h2_ref[0] = h2
    logits = lax.dot_general(rw_ref[...], h2.astype(BF16), _NT, preferred_element_type=F32) + rb_ref[...]
    iota = lax.broadcasted_iota(jnp.int32, logits.shape, 0)
    vals, idxs = [], []
    cur = logits
    for _ in range(TOP_K):
        mx = jnp.max(cur, axis=0, keepdims=True)
        ix = jnp.min(jnp.where(cur == mx, iota, N_EXPERTS), axis=0, keepdims=True)
        vals.append(mx)
        idxs.append(ix)
        cur = jnp.where(iota == ix, -jnp.inf, cur)
    tv = jnp.concatenate(vals, axis=0)
    e = jnp.exp(tv - tv[0:1])
    tw_ref[0, 0] = e / jnp.sum(e, axis=0, keepdims=True)
    tidx_ref[0, 0] = jnp.concatenate(idxs, axis=0)


def _merge(xs, modtab, oa, ob, oc, gates, woa_t, wob_t, woc_t, wout, gffn, rw_t, rb_col, n_ctx_tiles):
    bsz, ntok, d = xs.shape
    nt = ntok // TM
    const = lambda b, i: (0, 0)
    tile4 = lambda b, i: (b, i, 0, 0)
    tok3 = lambda b, i: (b, i, 0)
    return pl.pallas_call(
        _merge_kernel,
        grid=(bsz, nt),
        in_specs=[pl.BlockSpec((1, TM, d), tok3),
                  pl.BlockSpec((1, 1, N_MOD, d), lambda b, i: (b, jnp.where(i < n_ctx_tiles, 0, 1), 0, 0)),
                  pl.BlockSpec((1, 1, ATTN_WIDTH, TM), tile4),
                  pl.BlockSpec((1, 1, ATTN_WIDTH, TM), tile4),
                  pl.BlockSpec((1, 1, ATTN_WIDTH, TM), tile4),
                  pl.BlockSpec((1, 1, 3 * d, TM), tile4),
                  pl.BlockSpec(woa_t.shape, const),
                  pl.BlockSpec(wob_t.shape, const),
                  pl.BlockSpec(woc_t.shape, const),
                  pl.BlockSpec(wout.shape, const),
                  pl.BlockSpec((1, d), const),
                  pl.BlockSpec(rw_t.shape, const),
                  pl.BlockSpec(rb_col.shape, const)],
        out_specs=[pl.BlockSpec((1, TM, d), tok3),
                   pl.BlockSpec((1, TM, d), tok3),
                   pl.BlockSpec((1, 1, TOP_K, TM), tile4),
                   pl.BlockSpec((1, 1, TOP_K, TM), tile4)],
        out_shape=[jax.ShapeDtypeStruct((bsz, ntok, d), F32),
                   jax.ShapeDtypeStruct((bsz, ntok, d), F32),
                   jax.ShapeDtypeStruct((bsz, nt, TOP_K, TM), jnp.int32),
                   jax.ShapeDtypeStruct((bsz, nt, TOP_K, TM), F32)],
        compiler_params=_cparams(("arbitrary", "arbitrary")),
        name="merge_router",
    )(xs, modtab, oa, ob, oc, gates, woa_t, wob_t, woc_t, wout, gffn, rw_t, rb_col)


def _moe_kernel(be_ref, nv_ref, first_ref, idx_ref, x_hbm, wgu_ref, bgu_ref, wdn_ref, bdn_ref, sw_ref,
                y_hbm, xbuf, ybuf, wgu_bf, wdn_bf, gsem, ssem):
    i = pl.program_id(0)
    nv = nv_ref[i]
    d = D_MODEL

    def gather_copy(r):
        tok = lax.shift_right_logical(idx_ref[0, 0, r], 2)
        return pltpu.make_async_copy(x_hbm.at[pl.ds(tok, 1), :], xbuf.at[pl.ds(r, 1), :], gsem)

    def scatter_copy(r):
        return pltpu.make_async_copy(ybuf.at[pl.ds(r, 1), :], y_hbm.at[pl.ds(idx_ref[0, 0, r], 1), :], ssem)

    @pl.when(nv > 0)
    def _():
        @pl.when(nv < MOE_TB)
        def _():
            xbuf[...] = jnp.zeros_like(xbuf)

        def g_start(r, c):
            gather_copy(r).start()
            return c
        lax.fori_loop(0, nv, g_start, 0)

        @pl.when(first_ref[i] == 1)
        def _():
            wgu_bf[...] = wgu_ref[0, 0].astype(BF16)
            wdn_bf[...] = wdn_ref[0, 0].astype(BF16)

        def g_wait(r, c):
            gather_copy(r).wait()
            return c
        lax.fori_loop(0, nv, g_wait, 0)

        xb = xbuf[...].astype(BF16)
        gu = jnp.dot(xb, wgu_bf[...], preferred_element_type=F32) + bgu_ref[0, 0]
        glu = jnp.minimum(gu[:, :d], SWIGLU_LIMIT)
        lin = jnp.clip(gu[:, d:], -SWIGLU_LIMIT, SWIGLU_LIMIT)
        act = glu * jax.nn.sigmoid(SWIGLU_ALPHA * glu) * (lin + 1.0)
        y = jnp.dot(act.astype(BF16), wdn_bf[...], preferred_element_type=F32) + bdn_ref[0, 0]
        ybuf[...] = y * jnp.tile(sw_ref[...], (1, d // LANES))

        def s_start(r, c):
            scatter_copy(r).start()
            return c
        lax.fori_loop(0, nv, s_start, 0)

        def s_wait(r, c):
            scatter_copy(r).wait()
            return c
        lax.fori_loop(0, nv, s_wait, 0)


def _moe_experts(h2_flat, block_e, block_nv, block_first, slot_assign, slot_w128, w_gu, b_gu, w_dn, b_dn,
                 layer, n_assign):
    n_blocks = block_e.shape[0]
    d = D_MODEL
    grid_spec = pltpu.PrefetchScalarGridSpec(
        num_scalar_prefetch=3,
        grid=(n_blocks,),
        in_specs=[pl.BlockSpec((1, 1, MOE_TB), lambda i, be, nv, fs: (i, 0, 0), memory_space=pltpu.SMEM),
                  pl.BlockSpec(memory_space=pl.ANY),
                  pl.BlockSpec((1, 1, d, 2 * d), lambda i, be, nv, fs: (layer, be[i], 0, 0)),
                  pl.BlockSpec((1, 1, 1, 2 * d), lambda i, be, nv, fs: (layer, be[i], 0, 0)),
                  pl.BlockSpec((1, 1, d, d), lambda i, be, nv, fs: (layer, be[i], 0, 0)),
                  pl.BlockSpec((1, 1, 1, d), lambda i, be, nv, fs: (layer, be[i], 0, 0)),
                  pl.BlockSpec((MOE_TB, LANES), lambda i, be, nv, fs: (i, 0))],
        out_specs=pl.BlockSpec(memory_space=pl.ANY),
        scratch_shapes=[pltpu.VMEM((MOE_TB, d), F32),
                        pltpu.VMEM((MOE_TB, d), F32),
                        pltpu.VMEM((d, 2 * d), BF16),
                        pltpu.VMEM((d, d), BF16),
                        pltpu.SemaphoreType.DMA(()),
                        pltpu.SemaphoreType.DMA(())],
    )
    depth, ne = b_gu.shape[0], b_gu.shape[1]
    return pl.pallas_call(
        _moe_kernel,
        grid_spec=grid_spec,
        out_shape=jax.ShapeDtypeStruct((n_assign, d), F32),
        compiler_params=pltpu.CompilerParams(dimension_semantics=("arbitrary",), vmem_limit_bytes=VMEM_LIMIT,
                                             has_side_effects=True),
        name="moe_experts",
    )(block_e, block_nv, block_first, slot_assign.reshape(n_blocks, 1, MOE_TB), h2_flat,
      w_gu, b_gu.reshape(depth, ne, 1, 2 * d), w_dn, b_dn.reshape(depth, ne, 1, d), slot_w128)


def _moe_plan(top_idx, top_w):
    bsz, nt, _, tm = top_idx.shape
    n_tok = bsz * nt * tm
    n_assign = n_tok * TOP_K
    flat_e = jnp.transpose(top_idx, (0, 1, 3, 2)).reshape(-1)
    flat_w = jnp.transpose(top_w, (0, 1, 3, 2)).reshape(-1)
    order = jnp.argsort(flat_e, stable=True).astype(jnp.int32)
    e_sorted = flat_e[order]
    counts = jnp.bincount(flat_e, length=N_EXPERTS).astype(jnp.int32)
    starts = jnp.cumsum(counts) - counts
    padded = (counts + MOE_TB - 1) // MOE_TB * MOE_TB
    pad_ends = jnp.cumsum(padded)
    pad_starts = pad_ends - padded
    dest = pad_starts[e_sorted] + (jnp.arange(n_assign, dtype=jnp.int32) - starts[e_sorted])
    n_blocks = (n_assign + N_EXPERTS * (MOE_TB - 1) + MOE_TB - 1) // MOE_TB
    cap = n_blocks * MOE_TB
    slot_assign = jnp.full((cap,), 0, jnp.int32).at[dest].set(order)
    slot_w = jnp.zeros((cap,), F32).at[dest].set(flat_w[order])
    blk_start = jnp.arange(n_blocks, dtype=jnp.int32) * MOE_TB
    block_e = jnp.minimum(jnp.searchsorted(pad_ends, blk_start, side='right'), N_EXPERTS - 1).astype(jnp.int32)
    block_nv = jnp.clip(counts[block_e] - (blk_start - pad_starts[block_e]), 0, MOE_TB).astype(jnp.int32)
    block_first = ((blk_start == pad_starts[block_e]) & (block_nv > 0)).astype(jnp.int32)
    slot_w128 = jnp.broadcast_to(slot_w[:, None], (cap, LANES))
    return block_e, block_nv, block_first, slot_assign, slot_w128, n_assign


def _combine_kernel(x_ref, mod_ref, y_ref, o_ref):
    d = D_MODEL
    y = y_ref[0]
    tot = (y[:, 0:d] + y[:, d:2 * d]) + (y[:, 2 * d:3 * d] + y[:, 3 * d:4 * d])
    o_ref[0] = x_ref[0] + mod_ref[0, 0, 5:6, :] * tot


def _combine(xs, modtab, y, n_ctx_tiles, skip_tiles):
    bsz, ntok, d = xs.shape
    nt = ntok // TM
    n_out = nt - skip_tiles
    return pl.pallas_call(
        _combine_kernel,
        grid=(bsz, n_out),
        in_specs=[pl.BlockSpec((1, TM, d), lambda b, i: (b, i + skip_tiles, 0)),
                  pl.BlockSpec((1, 1, N_MOD, d),
                               lambda b, i: (b, jnp.where(i + skip_tiles < n_ctx_tiles, 0, 1), 0, 0)),
                  pl.BlockSpec((1, TM, TOP_K * d), lambda b, i: (b, i + skip_tiles, 0))],
        out_specs=pl.BlockSpec((1, TM, d), lambda b, i: (b, i, 0)),
        out_shape=jax.ShapeDtypeStruct((bsz, n_out * TM, d), F32),
        compiler_params=_cparams(("arbitrary", "arbitrary")),
        name="moe_combine",
    )(xs, modtab, y.reshape(bsz, ntok, TOP_K * d))


def _rope_table(n_ctx, n_lat):
    n_rows = n_lat // GRID_W
    row = jnp.repeat(jnp.arange(n_rows, dtype=F32), GRID_W)
    col = jnp.tile(jnp.arange(GRID_W, dtype=F32), n_rows)

    def table(rot_dim):
        axis_pairs = rot_dim // 4
        inv_freq = ROPE_THETA ** (-jnp.arange(axis_pairs, dtype=F32) / axis_pairs)
        ang = jnp.concatenate([row[:, None] * inv_freq, col[:, None] * inv_freq], axis=-1)
        cos = jnp.concatenate([jnp.ones((n_ctx, 2 * axis_pairs), F32), jnp.cos(ang)], axis=0)
        sin = jnp.concatenate([jnp.zeros((n_ctx, 2 * axis_pairs), F32), jnp.sin(ang)], axis=0)
        return cos, sin

    c64, s64 = table(A_DIM)
    c32, s32 = table(B_ROPE)
    tab = jnp.concatenate([c64, s64, c32, s32], axis=-1)
    ntok = n_ctx + n_lat
    return jnp.transpose(tab.reshape(ntok // TM, TM, tab.shape[1]), (0, 2, 1))


def _cols(v):
    return jnp.broadcast_to(v.astype(F32)[:, None], (v.shape[0], TM))


def kernel(x, c, ctx, c_ctx, w_mod, b_mod, norm_mix, norm_ffn, w_in, a_q_norm, a_k_norm, b_q_a_norm,
           b_kv_a_norm, b_w_uq, b_w_ukv, b_q_norm, b_k_norm, c_q_norm, c_k_norm, c_lambda, c_subln,
           w_o_a, w_o_b, w_o_c, w_out, router_w, router_b, exp_w_gu, exp_b_gu, exp_w_down, exp_b_down):
    bsz, n_lat, d = x.shape
    n_ctx = ctx.shape[1]
    depth = w_mod.shape[0]
    assert d == D_MODEL and n_ctx % TM == 0 and n_lat % TM == 0 and n_lat % GRID_W == 0 and bsz + 1 <= 8
    n_ctx_tiles = n_ctx // TM
    ntok = n_ctx + n_lat

    xs = jnp.concatenate([ctx, x], axis=1)
    cond8 = jnp.zeros((8, d), F32).at[:bsz].set(c).at[bsz].set(c_ctx)
    mod_all = _adaln(cond8, w_mod, b_mod).reshape(depth, 8, N_MOD, d)
    rope = _rope_table(n_ctx, n_lat)

    for l in range(depth):
        lam_init = 0.8 - 0.6 * math.exp(-0.3 * l)
        lq1, lk1, lq2, lk2 = c_lambda[l].astype(F32)
        lam = (jnp.exp(jnp.sum(lq1 * lk1)) - jnp.exp(jnp.sum(lq2 * lk2)) + lam_init).reshape(1)
        modtab = jnp.stack([jnp.broadcast_to(mod_all[l, bsz], (bsz, N_MOD, d)), mod_all[l, :bsz]], axis=1)
        gcol = jnp.concatenate([_cols(a_q_norm[l]), _cols(a_k_norm[l]), _cols(b_q_a_norm[l]),
                                _cols(b_kv_a_norm[l]), _cols(b_q_norm[l]), _cols(b_k_norm[l]),
                                _cols(c_q_norm[l]), _cols(c_k_norm[l])], axis=0)
        (qa, ka, va, qb, kb, vb, qc, kc, vc, gates) = _pre_mixer(
            xs, modtab, norm_mix[l].reshape(1, d), w_in[l].T.astype(BF16), b_w_uq[l].T.astype(BF16),
            b_w_ukv[l].T.astype(BF16), gcol, rope, n_ctx_tiles)

        gsub = _cols(c_subln[l])
        attn = functools.partial(_attention, lam, gsub=gsub, n_ctx_tiles=n_ctx_tiles)
        oa = attn(qa, ka, va, units=_UNITS_A, diff=False, out_scale=1.0, name="attn_gqa")
        ob = attn(qb, kb, vb, units=_UNITS_B, diff=False, out_scale=1.0, name="attn_mla")
        oc = attn(qc, kc, vc, units=_UNITS_C, diff=True, out_scale=1.0 - lam_init, name="attn_diff")

        xs, h2, top_idx, top_w = _merge(
            xs, modtab, oa, ob, oc, gates, w_o_a[l].T.astype(BF16), w_o_b[l].T.astype(BF16),
            w_o_c[l].T.astype(BF16), w_out[l].astype(BF16), norm_ffn[l].reshape(1, d),
            router_w[l].T.astype(BF16), _cols(router_b[l]), n_ctx_tiles)

        block_e, block_nv, block_first, slot_assign, slot_w128, n_assign = _moe_plan(top_idx, top_w)
        y = _moe_experts(h2.reshape(bsz * ntok, d), block_e, block_nv, block_first, slot_assign, slot_w128,
                         exp_w_gu, exp_b_gu, exp_w_down, exp_b_down, l, n_assign)
        last = l == depth - 1
        xs = _combine(xs, modtab, y.reshape(bsz, ntok * TOP_K, d), n_ctx_tiles, n_ctx_tiles if last else 0)
    return xs
```

```python
import functools
import math

import numpy as np
import jax
import jax.numpy as jnp
from jax import lax
from jax.experimental import pallas as pl
from jax.experimental.pallas import tpu as pltpu

F32 = jnp.float32
BF16 = jnp.bfloat16

D_MODEL = 1024
GRID_W = 64
ROPE_THETA = 10000.0
NORM_EPS = 1e-6
N_MOD = 6
A_HEADS, A_KV_HEADS, A_DIM = 8, 2, 64
B_HEADS, B_Q_RANK, B_KV_RANK, B_NOPE, B_ROPE, B_V = 8, 256, 128, 64, 32, 64
MLA_SCALE = (B_NOPE + B_ROPE) ** -0.5
LOG2E = math.log2(math.e)
C_HEADS, C_DIM, C_V = 8, 32, 64
N_EXPERTS, TOP_K = 32, 4
SWIGLU_ALPHA, SWIGLU_LIMIT = 1.702, 7.0
HEAD_V = 64
ATTN_WIDTH = 512

_IN_SIZES = (A_HEADS * A_DIM, A_KV_HEADS * A_DIM, A_KV_HEADS * A_DIM, B_Q_RANK, B_KV_RANK, B_ROPE,
             C_HEADS * 2 * C_DIM, C_HEADS * 2 * C_DIM, C_HEADS * C_V, 3 * D_MODEL)
_OFF = tuple(int(v) for v in np.cumsum((0,) + _IN_SIZES))
(_O_AQ, _O_AK, _O_AV, _O_BCQ, _O_BCKV, _O_BKR, _O_CQ, _O_CK, _O_CV, _O_GATE, _O_END) = _OFF
B_QW = B_NOPE + B_ROPE

_G_SIZES = (A_DIM, A_DIM, B_Q_RANK, B_KV_RANK, B_QW, B_QW, C_DIM, C_DIM)
_GOFF = tuple(int(v) for v in np.cumsum((0,) + _G_SIZES))
(_G_AQ, _G_AK, _G_BQA, _G_BKVA, _G_BQ, _G_BK, _G_CQ, _G_CK, _G_END) = _GOFF

LANES = 128
TM = 256
MOE_TB = 256
ATTN_GROUP = 8
ATTN_KS = 3
VMEM_LIMIT = 56 * 1024 * 1024

_NT = (((1,), (1,)), ((), ()))
_TN = (((0,), (0,)), ((), ()))


def _cparams(sem):
    return pltpu.CompilerParams(dimension_semantics=sem, vmem_limit_bytes=VMEM_LIMIT)


def _adaln_kernel(cond_ref, w_ref, b_ref, o_ref):
    cnd = cond_ref[...]
    s = (cnd * jax.nn.sigmoid(cnd)).astype(BF16)
    o_ref[0] = jnp.dot(s, w_ref[0].astype(BF16), preferred_element_type=F32) + b_ref[0]


def _adaln(cond8, w_mod, b_mod):
    depth, d, nm = w_mod.shape
    tn = 1536
    return pl.pallas_call(
        _adaln_kernel,
        grid=(depth, nm // tn),
        in_specs=[pl.BlockSpec((8, d), lambda l, j: (0, 0)),
                  pl.BlockSpec((1, d, tn), lambda l, j: (l, 0, j)),
                  pl.BlockSpec((1, 1, tn), lambda l, j: (l, 0, j))],
        out_specs=pl.BlockSpec((1, 8, tn), lambda l, j: (l, 0, j)),
        out_shape=jax.ShapeDtypeStruct((depth, 8, nm), F32),
        compiler_params=_cparams(("arbitrary", "arbitrary")),
        name="adaln",
    )(cond8, w_mod, b_mod.reshape(depth, 1, nm))


def _rms_tokens(x, gain):
    ms = jnp.mean(x * x, axis=-1, keepdims=True)
    return x * lax.rsqrt(ms + NORM_EPS) * gain


def _rms_rows(x, g):
    ms = jnp.mean(x * x, axis=0, keepdims=True)
    return x * lax.rsqrt(ms + NORM_EPS) * g


def _rope_rows(y, cos, sin):
    r = y.shape[0] // 2
    y1, y2 = y[:r], y[r:]
    return y1 * cos - y2 * sin, y2 * cos + y1 * sin


def _pre_kernel(x_ref, mod_ref, gmix_ref, w_ref, wuq_ref, wukv_ref, gcol_ref, rope_ref,
                qa_ref, ka_ref, va_ref, qb_ref, kb_ref, vb_ref, qc_ref, kc_ref, vc_ref, gates_ref):
    x = x_ref[0]
    sh1 = mod_ref[0, 0, 0:1, :]
    sc1 = mod_ref[0, 0, 1:2, :]
    h = _rms_tokens(x, gmix_ref[...]) * (1.0 + sc1) + sh1
    hb = h.astype(BF16)
    proj = lax.dot_general(w_ref[0:_O_GATE, :], hb, _NT, preferred_element_type=F32)
    gates = lax.dot_general(w_ref[_O_GATE:_O_END, :], hb, _NT, preferred_element_type=F32)
    gates_ref[0, 0] = jax.nn.sigmoid(gates)

    cos64, sin64 = rope_ref[0, 0:32, :], rope_ref[0, 32:64, :]
    cos32, sin32 = rope_ref[0, 64:80, :], rope_ref[0, 80:96, :]

    def gcol(a, b):
        return gcol_ref[a:b, :]

    g_aq, g_ak = gcol(_G_AQ, _G_AK), gcol(_G_AK, _G_BQA)
    a_scale = A_DIM ** -0.5 * LOG2E
    for hd in range(A_HEADS):
        r0 = _O_AQ + hd * A_DIM
        o1, o2 = _rope_rows(_rms_rows(proj[r0:r0 + A_DIM], g_aq), cos64, sin64)
        qa_ref[0, 0, hd * A_DIM:hd * A_DIM + 32, :] = (o1 * a_scale).astype(BF16)
        qa_ref[0, 0, hd * A_DIM + 32:(hd + 1) * A_DIM, :] = (o2 * a_scale).astype(BF16)
    k_parts = []
    for g in range(A_KV_HEADS):
        r0 = _O_AK + g * A_DIM
        o1, o2 = _rope_rows(_rms_rows(proj[r0:r0 + A_DIM], g_ak), cos64, sin64)
        k_parts += [o1, o2]
    ka_ref[0] = jnp.concatenate(k_parts, axis=0).T.astype(BF16)
    va_ref[0, 0] = proj[_O_AV:_O_BCQ].astype(BF16)

    g_bq, g_bk = gcol(_G_BQ, _G_BK), gcol(_G_BK, _G_CQ)
    b_scale = MLA_SCALE * LOG2E
    cqn = _rms_rows(proj[_O_BCQ:_O_BCKV], gcol(_G_BQA, _G_BKVA)).astype(BF16)
    bq = jnp.dot(wuq_ref[...], cqn, preferred_element_type=F32)
    for hd in range(B_HEADS):
        r0 = hd * B_QW
        qn = _rms_rows(bq[r0:r0 + B_NOPE], g_bq[:B_NOPE])
        qr = _rms_rows(bq[r0 + B_NOPE:r0 + B_QW], g_bq[B_NOPE:])
        r1, r2 = _rope_rows(qr, cos32, sin32)
        qb_ref[0, 0, r0:r0 + B_NOPE, :] = (qn * b_scale).astype(BF16)
        qb_ref[0, 0, r0 + B_NOPE:r0 + B_NOPE + 16, :] = (r1 * b_scale).astype(BF16)
        qb_ref[0, 0, r0 + B_NOPE + 16:r0 + B_QW, :] = (r2 * b_scale).astype(BF16)
    ckvn = _rms_rows(proj[_O_BCKV:_O_BKR], gcol(_G_BKVA, _G_BQ)).astype(BF16)
    bkv = jnp.dot(wukv_ref[...], ckvn, preferred_element_type=F32)
    kr = _rms_rows(proj[_O_BKR:_O_CQ], g_bk[B_NOPE:])
    kr1, kr2 = _rope_rows(kr, cos32, sin32)
    zpad = jnp.zeros((LANES - B_QW, kr1.shape[1]), F32)
    k_parts = []
    for hd in range(B_HEADS):
        r0 = hd * (B_NOPE + B_V)
        kn = _rms_rows(bkv[r0:r0 + B_NOPE], g_bk[:B_NOPE])
        k_parts += [kn, kr1, kr2, zpad]
        vb_ref[0, 0, hd * B_V:(hd + 1) * B_V, :] = bkv[r0 + B_NOPE:r0 + B_NOPE + B_V].astype(BF16)
    kb_ref[0] = jnp.concatenate(k_parts, axis=0).T.astype(BF16)

    g_cq, g_ck = gcol(_G_CQ, _G_CK), gcol(_G_CK, _G_END)
    c_scale = C_DIM ** -0.5 * LOG2E
    k_parts = []
    for j in range(2 * C_HEADS):
        r0 = _O_CQ + j * C_DIM
        o1, o2 = _rope_rows(_rms_rows(proj[r0:r0 + C_DIM], g_cq), cos32, sin32)
        qc_ref[0, 0, j * C_DIM:j * C_DIM + 16, :] = (o1 * c_scale).astype(BF16)
        qc_ref[0, 0, j * C_DIM + 16:(j + 1) * C_DIM, :] = (o2 * c_scale).astype(BF16)
        r0 = _O_CK + j * C_DIM
        o1, o2 = _rope_rows(_rms_rows(proj[r0:r0 + C_DIM], g_ck), cos32, sin32)
        k_parts += [o1, o2]
    kc_ref[0] = jnp.concatenate(k_parts, axis=0).T.astype(BF16)
    vc_ref[0, 0] = proj[_O_CV:_O_GATE].astype(BF16)


def _pre_mixer(xs, modtab, gmix, w_int, wuq_t, wukv_t, gcol, rope, n_ctx_tiles):
    bsz, ntok, d = xs.shape
    nt = ntok // TM

    def fm(rows, dtype=BF16):
        return (jax.ShapeDtypeStruct((bsz, nt, rows, TM), dtype),
                pl.BlockSpec((1, 1, rows, TM), lambda b, i: (b, i, 0, 0)))

    def tmaj(cols):
        return (jax.ShapeDtypeStruct((bsz, ntok, cols), BF16),
                pl.BlockSpec((1, TM, cols), lambda b, i: (b, i, 0)))

    outs = [fm(A_HEADS * A_DIM), tmaj(LANES), fm(A_KV_HEADS * A_DIM),
            fm(B_HEADS * B_QW), tmaj(B_HEADS * LANES), fm(B_HEADS * B_V),
            fm(2 * C_HEADS * C_DIM), tmaj(2 * C_HEADS * C_DIM), fm(C_HEADS * C_V),
            fm(3 * D_MODEL, F32)]
    const = lambda b, i: (0, 0)
    return pl.pallas_call(
        _pre_kernel,
        grid=(bsz, nt),
        in_specs=[pl.BlockSpec((1, TM, d), lambda b, i: (b, i, 0)),
                  pl.BlockSpec((1, 1, N_MOD, d), lambda b, i: (b, jnp.where(i < n_ctx_tiles, 0, 1), 0, 0)),
                  pl.BlockSpec((1, d), const),
                  pl.BlockSpec(w_int.shape, const),
                  pl.BlockSpec(wuq_t.shape, const),
                  pl.BlockSpec(wukv_t.shape, const),
                  pl.BlockSpec(gcol.shape, const),
                  pl.BlockSpec((1, rope.shape[1], TM), lambda b, i: (i, 0, 0))],
        out_specs=[o[1] for o in outs],
        out_shape=[o[0] for o in outs],
        compiler_params=_cparams(("arbitrary", "arbitrary")),
        name="pre_mixer",
    )(xs, modtab, gmix, w_int, wuq_t, wukv_t, gcol, rope)


def _attn_kernel(lam_ref, q_ref, k_ref, v_ref, gsub_ref, o_ref, *, units, group_size, diff, out_scale,
                 n_ctx_tiles, n_tiles):
    i = pl.program_id(1)
    tq = q_ref.shape[3]
    lat_ks = ATTN_KS if n_tiles % ATTN_KS == 0 else 1

    def padded_q(q_row0, q_rows, pad_off):
        q = q_ref[0, 0, q_row0:q_row0 + q_rows, :]
        parts = []
        if pad_off:
            parts.append(jnp.zeros((pad_off, tq), BF16))
        parts.append(q)
        rest = LANES - pad_off - q_rows
        if rest:
            parts.append(jnp.zeros((rest, tq), BF16))
        return jnp.concatenate(parts, axis=0) if len(parts) > 1 else q

    def run_group(group, ks, n_steps):
        qpads = [padded_q(u[0], u[1], u[2]) for u in group]

        def qk(step):
            row0 = pl.multiple_of(step * (ks * TM), ks * TM)
            out = []
            for (_, _, _, k_blk, _), qpad in zip(group, qpads):
                k = k_ref[0, pl.ds(row0, ks * TM), k_blk * LANES:(k_blk + 1) * LANES]
                out.append(jnp.dot(k, qpad, preferred_element_type=F32))
            return tuple(out)

        def body(step, state):
            scores = qk(step)
            new = []
            for (_, _, _, _, v_row0), s, (m, l, acc) in zip(group, scores, state):
                m_new = jnp.maximum(m, jnp.max(s, axis=0, keepdims=True))
                alpha = jnp.exp2(m - m_new)
                p = jnp.exp2(s - m_new)
                l = alpha * l + jnp.sum(p, axis=0, keepdims=True)
                pb = p.astype(BF16)
                pv = None
                for t in range(ks):
                    v = v_ref[0, step * ks + t, v_row0:v_row0 + HEAD_V, :]
                    d = jnp.dot(v, pb[t * TM:(t + 1) * TM], preferred_element_type=F32)
                    pv = d if pv is None else pv + d
                new.append((m_new, l, alpha * acc + pv))
            return tuple(new)

        init = tuple((jnp.full((1, tq), -1e30, F32), jnp.zeros((1, tq), F32), jnp.zeros((HEAD_V, tq), F32))
                     for _ in group)
        return [acc / l for (_, l, acc) in lax.fori_loop(0, n_steps, body, init)]

    def run_tile(ks, n_steps):
        per_head = 2 if diff else 1
        for g0 in range(0, len(units), group_size):
            outs = run_group(units[g0:g0 + group_size], ks, n_steps)
            for j in range(0, group_size, per_head):
                hd = (g0 + j) // per_head
                if diff:
                    o = outs[j] - lam_ref[0] * outs[j + 1]
                    o = _rms_rows(o, gsub_ref[...]) * out_scale
                else:
                    o = outs[j]
                o_ref[0, 0, hd * HEAD_V:(hd + 1) * HEAD_V, :] = o.astype(BF16)

    @pl.when(i < n_ctx_tiles)
    def _():
        run_tile(1, n_ctx_tiles)

    @pl.when(i >= n_ctx_tiles)
    def _():
        run_tile(lat_ks, n_tiles // lat_ks)


def _attention(lam, qt, k, vt, gsub, *, units, diff, out_scale, n_ctx_tiles, name):
    bsz, nt, qrows, _ = qt.shape
    ntok, kcols = k.shape[1], k.shape[2]
    vrows = vt.shape[2]
    kern = functools.partial(_attn_kernel, units=units, group_size=ATTN_GROUP, diff=diff, out_scale=out_scale,
                             n_ctx_tiles=n_ctx_tiles, n_tiles=nt)
    return pl.pallas_call(
        kern,
        grid=(bsz, nt),
        in_specs=[pl.BlockSpec(memory_space=pltpu.SMEM),
                  pl.BlockSpec((1, 1, qrows, TM), lambda b, i: (b, i, 0, 0)),
                  pl.BlockSpec((1, ntok, kcols), lambda b, i: (b, 0, 0), pipeline_mode=pl.Buffered(1)),
                  pl.BlockSpec((1, nt, vrows, TM), lambda b, i: (b, 0, 0, 0), pipeline_mode=pl.Buffered(1)),
                  pl.BlockSpec(gsub.shape, lambda b, i: (0, 0))],
        out_specs=pl.BlockSpec((1, 1, ATTN_WIDTH, TM), lambda b, i: (b, i, 0, 0)),
        out_shape=jax.ShapeDtypeStruct((bsz, nt, ATTN_WIDTH, TM), BF16),
        compiler_params=_cparams(("arbitrary", "arbitrary")),
        name=name,
    )(lam, qt, k, vt, gsub)


_UNITS_A = tuple((h * A_DIM, A_DIM, A_DIM * (h // (A_HEADS // A_KV_HEADS)), 0,
                  HEAD_V * (h // (A_HEADS // A_KV_HEADS))) for h in range(A_HEADS))
_UNITS_B = tuple((h * B_QW, B_QW, 0, h, h * B_V) for h in range(B_HEADS))
_UNITS_C = tuple((h * 2 * C_DIM + c * C_DIM, C_DIM, C_DIM * (2 * (h % 2) + c), h // 2, h * C_V)
                 for h in range(C_HEADS) for c in range(2))


def _merge_kernel(x_ref, mod_ref, oa_ref, ob_ref, oc_ref, gates_ref, woa_ref, wob_ref, woc_ref, wout_ref,
                  gffn_ref, rw_ref, rb_ref, xo_ref, h2_ref, tidx_ref, tw_ref):
    d = D_MODEL
    ya = jnp.dot(woa_ref[...], oa_ref[0, 0], preferred_element_type=F32)
    yb = jnp.dot(wob_ref[...], ob_ref[0, 0], preferred_element_type=F32)
    yc = jnp.dot(woc_ref[...], oc_ref[0, 0], preferred_element_type=F32)
    yt = (gates_ref[0, 0, 0:d, :] * ya + gates_ref[0, 0, d:2 * d, :] * yb
          + gates_ref[0, 0, 2 * d:3 * d, :] * yc)
    out = lax.dot_general(yt.astype(BF16), wout_ref[...], _TN, preferred_element_type=F32)
    g1 = mod_ref[0, 0, 2:3, :]
    x = x_ref[0] + g1 * out
    xo_ref[0] = x
    sh2, sc2 = mod_ref[0, 0, 3:4, :], mod_ref[0, 0, 4:5, :]
    h2 = _rms_tokens(x, gffn_ref[...]) * (1.0 + sc2) + sh2
    h2_ref[0] = h2
    logits = lax.dot_general(rw_ref[...], h2.astype(BF16), _NT, preferred_element_type=F32) + rb_ref[...]
    iota = lax.broadcasted_iota(jnp.int32, logits.shape, 0)
    vals, idxs = [], []
    cur = logits
    for _ in range(TOP_K):
        mx = jnp.max(cur, axis=0, keepdims=True)
        ix = jnp.min(jnp.where(cur == mx, iota, N_EXPERTS), axis=0, keepdims=True)
        vals.append(mx)
        idxs.append(ix)
        cur = jnp.where(iota == ix, -jnp.inf, cur)
    tv = jnp.concatenate(vals, axis=0)
    e = jnp.exp(tv - tv[0:1])
    tw_ref[0, 0] = e / jnp.sum(e, axis=0, keepdims=True)
    tidx_ref[0, 0] = jnp.concatenate(idxs, axis=0)


def _merge(xs, modtab, oa, ob, oc, gates, woa_t, wob_t, woc_t, wout, gffn, rw_t, rb_col, n_ctx_tiles):
    bsz, ntok, d = xs.shape
    nt = ntok // TM
    const = lambda b, i: (0, 0)
    tile4 = lambda b, i: (b, i, 0, 0)
    tok3 = lambda b, i: (b, i, 0)
    return pl.pallas_call(
        _merge_kernel,
        grid=(bsz, nt),
        in_specs=[pl.BlockSpec((1, TM, d), tok3),
                  pl.BlockSpec((1, 1, N_MOD, d), lambda b, i: (b, jnp.where(i < n_ctx_tiles, 0, 1), 0, 0)),
                  pl.BlockSpec((1, 1, ATTN_WIDTH, TM), tile4),
                  pl.BlockSpec((1, 1, ATTN_WIDTH, TM), tile4),
                  pl.BlockSpec((1, 1, ATTN_WIDTH, TM), tile4),
                  pl.BlockSpec((1, 1, 3 * d, TM), tile4),
                  pl.BlockSpec(woa_t.shape, const),
                  pl.BlockSpec(wob_t.shape, const),
                  pl.BlockSpec(woc_t.shape, const),
                  pl.BlockSpec(wout.shape, const),
                  pl.BlockSpec((1, d), const),
                  pl.BlockSpec(rw_t.shape, const),
                  pl.BlockSpec(rb_col.shape, const)],
        out_specs=[pl.BlockSpec((1, TM, d), tok3),
                   pl.BlockSpec((1, TM, d), tok3),
                   pl.BlockSpec((1, 1, TOP_K, TM), tile4),
                   pl.BlockSpec((1, 1, TOP_K, TM), tile4)],
        out_shape=[jax.ShapeDtypeStruct((bsz, ntok, d), F32),
                   jax.ShapeDtypeStruct((bsz, ntok, d), F32),
                   jax.ShapeDtypeStruct((bsz, nt, TOP_K, TM), jnp.int32),
                   jax.ShapeDtypeStruct((bsz, nt, TOP_K, TM), F32)],
        compiler_params=_cparams(("arbitrary", "arbitrary")),
        name="merge_router",
    )(xs, modtab, oa, ob, oc, gates, woa_t, wob_t, woc_t, wout, gffn, rw_t, rb_col)


def _moe_kernel(be_ref, nv_ref, first_ref, idx_ref, x_hbm, wgu_ref, bgu_ref, wdn_ref, bdn_ref, sw_ref,
                y_hbm, xbuf, ybuf, wgu_bf, wdn_bf, gsem, ssem, *, n_tok):
    i = pl.program_id(0)
    nv = nv_ref[i]
    d = D_MODEL

    def gather_copy(r):
        tok = lax.shift_right_logical(idx_ref[0, 0, r], 2)
        return pltpu.make_async_copy(x_hbm.at[pl.ds(tok, 1), :], xbuf.at[pl.ds(r, 1), :], gsem)

    def scatter_copy(r):
        a = idx_ref[0, 0, r]
        row = (a & (TOP_K - 1)) * n_tok + lax.shift_right_logical(a, 2)
        return pltpu.make_async_copy(ybuf.at[pl.ds(r, 1), :], y_hbm.at[pl.ds(row, 1), :], ssem)

    @pl.when(nv > 0)
    def _():
        @pl.when(nv < MOE_TB)
        def _():
            xbuf[...] = jnp.zeros_like(xbuf)

        def g_start(r, c):
            gather_copy(r).start()
            return c
        lax.fori_loop(0, nv, g_start, 0)

        @pl.when(first_ref[i] == 1)
        def _():
            wgu_bf[...] = wgu_ref[0, 0].astype(BF16)
            wdn_bf[...] = wdn_ref[0, 0].astype(BF16)

        def g_wait(r, c):
            gather_copy(r).wait()
            return c
        lax.fori_loop(0, nv, g_wait, 0)

        xb = xbuf[...].astype(BF16)
        gu = jnp.dot(xb, wgu_bf[...], preferred_element_type=F32) + bgu_ref[0, 0]
        glu = jnp.minimum(gu[:, :d], SWIGLU_LIMIT)
        lin = jnp.clip(gu[:, d:], -SWIGLU_LIMIT, SWIGLU_LIMIT)
        act = glu * jax.nn.sigmoid(SWIGLU_ALPHA * glu) * (lin + 1.0)
        y = jnp.dot(act.astype(BF16), wdn_bf[...], preferred_element_type=F32) + bdn_ref[0, 0]
        ybuf[...] = y * jnp.tile(sw_ref[...], (1, d // LANES))

        def s_start(r, c):
            scatter_copy(r).start()
            return c
        lax.fori_loop(0, nv, s_start, 0)

        def s_wait(r, c):
            scatter_copy(r).wait()
            return c
        lax.fori_loop(0, nv, s_wait, 0)


def _moe_experts(h2_flat, block_e, block_nv, block_first, slot_assign, slot_w128, w_gu, b_gu, w_dn, b_dn,
                 layer, n_assign):
    n_blocks = block_e.shape[0]
    d = D_MODEL
    grid_spec = pltpu.PrefetchScalarGridSpec(
        num_scalar_prefetch=3,
        grid=(n_blocks,),
        in_specs=[pl.BlockSpec((1, 1, MOE_TB), lambda i, be, nv, fs: (i, 0, 0), memory_space=pltpu.SMEM),
                  pl.BlockSpec(memory_space=pl.ANY),
                  pl.BlockSpec((1, 1, d, 2 * d), lambda i, be, nv, fs: (layer, be[i], 0, 0)),
                  pl.BlockSpec((1, 1, 1, 2 * d), lambda i, be, nv, fs: (layer, be[i], 0, 0)),
                  pl.BlockSpec((1, 1, d, d), lambda i, be, nv, fs: (layer, be[i], 0, 0)),
                  pl.BlockSpec((1, 1, 1, d), lambda i, be, nv, fs: (layer, be[i], 0, 0)),
                  pl.BlockSpec((MOE_TB, LANES), lambda i, be, nv, fs: (i, 0))],
        out_specs=pl.BlockSpec(memory_space=pl.ANY),
        scratch_shapes=[pltpu.VMEM((MOE_TB, d), F32),
                        pltpu.VMEM((MOE_TB, d), F32),
                        pltpu.VMEM((d, 2 * d), BF16),
                        pltpu.VMEM((d, d), BF16),
                        pltpu.SemaphoreType.DMA(()),
                        pltpu.SemaphoreType.DMA(())],
    )
    depth, ne = b_gu.shape[0], b_gu.shape[1]
    return pl.pallas_call(
        functools.partial(_moe_kernel, n_tok=n_assign // TOP_K),
        grid_spec=grid_spec,
        out_shape=jax.ShapeDtypeStruct((n_assign, d), F32),
        compiler_params=pltpu.CompilerParams(dimension_semantics=("arbitrary",), vmem_limit_bytes=VMEM_LIMIT,
                                             has_side_effects=True),
        name="moe_experts",
    )(block_e, block_nv, block_first, slot_assign.reshape(n_blocks, 1, MOE_TB), h2_flat,
      w_gu, b_gu.reshape(depth, ne, 1, 2 * d), w_dn, b_dn.reshape(depth, ne, 1, d), slot_w128)


def _moe_plan(top_idx, top_w):
    bsz, nt, _, tm = top_idx.shape
    n_tok = bsz * nt * tm
    n_assign = n_tok * TOP_K
    flat_e = jnp.transpose(top_idx, (0, 1, 3, 2)).reshape(-1)
    flat_w = jnp.transpose(top_w, (0, 1, 3, 2)).reshape(-1)
    order = jnp.argsort(flat_e, stable=True).astype(jnp.int32)
    e_sorted = flat_e[order]
    counts = jnp.bincount(flat_e, length=N_EXPERTS).astype(jnp.int32)
    starts = jnp.cumsum(counts) - counts
    padded = (counts + MOE_TB - 1) // MOE_TB * MOE_TB
    pad_ends = jnp.cumsum(padded)
    pad_starts = pad_ends - padded
    dest = pad_starts[e_sorted] + (jnp.arange(n_assign, dtype=jnp.int32) - starts[e_sorted])
    n_blocks = (n_assign + N_EXPERTS * (MOE_TB - 1) + MOE_TB - 1) // MOE_TB
    cap = n_blocks * MOE_TB
    slot_assign = jnp.full((cap,), 0, jnp.int32).at[dest].set(order)
    slot_w = jnp.zeros((cap,), F32).at[dest].set(flat_w[order])
    blk_start = jnp.arange(n_blocks, dtype=jnp.int32) * MOE_TB
    block_e = jnp.minimum(jnp.searchsorted(pad_ends, blk_start, side='right'), N_EXPERTS - 1).astype(jnp.int32)
    block_nv = jnp.clip(counts[block_e] - (blk_start - pad_starts[block_e]), 0, MOE_TB).astype(jnp.int32)
    block_first = ((blk_start == pad_starts[block_e]) & (block_nv > 0)).astype(jnp.int32)
    slot_w128 = jnp.broadcast_to(slot_w[:, None], (cap, LANES))
    return block_e, block_nv, block_first, slot_assign, slot_w128, n_assign


def _combine_kernel(x_ref, mod_ref, y_ref, o_ref):
    tot = (y_ref[0, 0] + y_ref[1, 0]) + (y_ref[2, 0] + y_ref[3, 0])
    o_ref[0] = x_ref[0] + mod_ref[0, 0, 5:6, :] * tot


def _combine(xs, modtab, y, n_ctx_tiles, skip_tiles):
    bsz, ntok, d = xs.shape
    nt = ntok // TM
    n_out = nt - skip_tiles
    return pl.pallas_call(
        _combine_kernel,
        grid=(bsz, n_out),
        in_specs=[pl.BlockSpec((1, TM, d), lambda b, i: (b, i + skip_tiles, 0)),
                  pl.BlockSpec((1, 1, N_MOD, d),
                               lambda b, i: (b, jnp.where(i + skip_tiles < n_ctx_tiles, 0, 1), 0, 0)),
                  pl.BlockSpec((TOP_K, 1, TM, d), lambda b, i: (0, b, i + skip_tiles, 0))],
        out_specs=pl.BlockSpec((1, TM, d), lambda b, i: (b, i, 0)),
        out_shape=jax.ShapeDtypeStruct((bsz, n_out * TM, d), F32),
        compiler_params=_cparams(("arbitrary", "arbitrary")),
        name="moe_combine",
    )(xs, modtab, y.reshape(TOP_K, bsz, ntok, d))


def _rope_table(n_ctx, n_lat):
    n_rows = n_lat // GRID_W
    row = jnp.repeat(jnp.arange(n_rows, dtype=F32), GRID_W)
    col = jnp.tile(jnp.arange(GRID_W, dtype=F32), n_rows)

    def table(rot_dim):
        axis_pairs = rot_dim // 4
        inv_freq = ROPE_THETA ** (-jnp.arange(axis_pairs, dtype=F32) / axis_pairs)
        ang = jnp.concatenate([row[:, None] * inv_freq, col[:, None] * inv_freq], axis=-1)
        cos = jnp.concatenate([jnp.ones((n_ctx, 2 * axis_pairs), F32), jnp.cos(ang)], axis=0)
        sin = jnp.concatenate([jnp.zeros((n_ctx, 2 * axis_pairs), F32), jnp.sin(ang)], axis=0)
        return cos, sin

    c64, s64 = table(A_DIM)
    c32, s32 = table(B_ROPE)
    tab = jnp.concatenate([c64, s64, c32, s32], axis=-1)
    ntok = n_ctx + n_lat
    return jnp.transpose(tab.reshape(ntok // TM, TM, tab.shape[1]), (0, 2, 1))


def _cols(v):
    return jnp.broadcast_to(v.astype(F32)[:, None], (v.shape[0], TM))


def kernel(x, c, ctx, c_ctx, w_mod, b_mod, norm_mix, norm_ffn, w_in, a_q_norm, a_k_norm, b_q_a_norm,
           b_kv_a_norm, b_w_uq, b_w_ukv, b_q_norm, b_k_norm, c_q_norm, c_k_norm, c_lambda, c_subln,
           w_o_a, w_o_b, w_o_c, w_out, router_w, router_b, exp_w_gu, exp_b_gu, exp_w_down, exp_b_down):
    bsz, n_lat, d = x.shape
    n_ctx = ctx.shape[1]
    depth = w_mod.shape[0]
    assert d == D_MODEL and n_ctx % TM == 0 and n_lat % TM == 0 and n_lat % GRID_W == 0 and bsz + 1 <= 8
    n_ctx_tiles = n_ctx // TM
    ntok = n_ctx + n_lat

    xs = jnp.concatenate([ctx, x], axis=1)
    cond8 = jnp.zeros((8, d), F32).at[:bsz].set(c).at[bsz].set(c_ctx)
    mod_all = _adaln(cond8, w_mod, b_mod).reshape(depth, 8, N_MOD, d)
    rope = _rope_table(n_ctx, n_lat)

    for l in range(depth):
        lam_init = 0.8 - 0.6 * math.exp(-0.3 * l)
        lq1, lk1, lq2, lk2 = c_lambda[l].astype(F32)
        lam = (jnp.exp(jnp.sum(lq1 * lk1)) - jnp.exp(jnp.sum(lq2 * lk2)) + lam_init).reshape(1)
        modtab = jnp.stack([jnp.broadcast_to(mod_all[l, bsz], (bsz, N_MOD, d)), mod_all[l, :bsz]], axis=1)
        gcol = jnp.concatenate([_cols(a_q_norm[l]), _cols(a_k_norm[l]), _cols(b_q_a_norm[l]),
                                _cols(b_kv_a_norm[l]), _cols(b_q_norm[l]), _cols(b_k_norm[l]),
                                _cols(c_q_norm[l]), _cols(c_k_norm[l])], axis=0)
        (qa, ka, va, qb, kb, vb, qc, kc, vc, gates) = _pre_mixer(
            xs, modtab, norm_mix[l].reshape(1, d), w_in[l].T.astype(BF16), b_w_uq[l].T.astype(BF16),
            b_w_ukv[l].T.astype(BF16), gcol, rope, n_ctx_tiles)

        gsub = _cols(c_subln[l])
        attn = functools.partial(_attention, lam, gsub=gsub, n_ctx_tiles=n_ctx_tiles)
        oa = attn(qa, ka, va, units=_UNITS_A, diff=False, out_scale=1.0, name="attn_gqa")
        ob = attn(qb, kb, vb, units=_UNITS_B, diff=False, out_scale=1.0, name="attn_mla")
        oc = attn(qc, kc, vc, units=_UNITS_C, diff=True, out_scale=1.0 - lam_init, name="attn_diff")

        xs, h2, top_idx, top_w = _merge(
            xs, modtab, oa, ob, oc, gates, w_o_a[l].T.astype(BF16), w_o_b[l].T.astype(BF16),
            w_o_c[l].T.astype(BF16), w_out[l].astype(BF16), norm_ffn[l].reshape(1, d),
            router_w[l].T.astype(BF16), _cols(router_b[l]), n_ctx_tiles)

        block_e, block_nv, block_first, slot_assign, slot_w128, n_assign = _moe_plan(top_idx, top_w)
        y = _moe_experts(h2.reshape(bsz * ntok, d), block_e, block_nv, block_first, slot_assign, slot_w128,
                         exp_w_gu, exp_b_gu, exp_w_down, exp_b_down, l, n_assign)
        last = l == depth - 1
        xs = _combine(xs, modtab, y, n_ctx_tiles, n_ctx_tiles if last else 0)
    return xs
```

```python
import functools
import math

import numpy as np
import jax
import jax.numpy as jnp
from jax import lax
from jax.experimental import pallas as pl
from jax.experimental.pallas import tpu as pltpu

F32 = jnp.float32
BF16 = jnp.bfloat16

D_MODEL = 1024
GRID_W = 64
ROPE_THETA = 10000.0
NORM_EPS = 1e-6
N_MOD = 6
A_HEADS, A_KV_HEADS, A_DIM = 8, 2, 64
B_HEADS, B_Q_RANK, B_KV_RANK, B_NOPE, B_ROPE, B_V = 8, 256, 128, 64, 32, 64
MLA_SCALE = (B_NOPE + B_ROPE) ** -0.5
LOG2E = math.log2(math.e)
C_HEADS, C_DIM, C_V = 8, 32, 64
N_EXPERTS, TOP_K = 32, 4
SWIGLU_ALPHA, SWIGLU_LIMIT = 1.702, 7.0
HEAD_V = 64
ATTN_WIDTH = 512

_IN_SIZES = (A_HEADS * A_DIM, A_KV_HEADS * A_DIM, A_KV_HEADS * A_DIM, B_Q_RANK, B_KV_RANK, B_ROPE,
             C_HEADS * 2 * C_DIM, C_HEADS * 2 * C_DIM, C_HEADS * C_V, 3 * D_MODEL)
_OFF = tuple(int(v) for v in np.cumsum((0,) + _IN_SIZES))
(_O_AQ, _O_AK, _O_AV, _O_BCQ, _O_BCKV, _O_BKR, _O_CQ, _O_CK, _O_CV, _O_GATE, _O_END) = _OFF
B_QW = B_NOPE + B_ROPE

_G_SIZES = (A_DIM, A_DIM, B_Q_RANK, B_KV_RANK, B_QW, B_QW, C_DIM, C_DIM)
_GOFF = tuple(int(v) for v in np.cumsum((0,) + _G_SIZES))
(_G_AQ, _G_AK, _G_BQA, _G_BKVA, _G_BQ, _G_BK, _G_CQ, _G_CK, _G_END) = _GOFF

LANES = 128
TM = 256
MOE_TB = 256
MOE_CW = 256
_TM_SHIFT = TM.bit_length() - 1
_K_SHIFT = TOP_K.bit_length() - 1
assert TM == 1 << _TM_SHIFT and TOP_K == 1 << _K_SHIFT
ATTN_GROUP = 8
ATTN_AHEAD = 2
ATTN_LAG = 1
ATTN_KS = 3
VMEM_LIMIT = 56 * 1024 * 1024

_NT = (((1,), (1,)), ((), ()))
_TN = (((0,), (0,)), ((), ()))


def _cparams(sem):
    return pltpu.CompilerParams(dimension_semantics=sem, vmem_limit_bytes=VMEM_LIMIT)


def _adaln_kernel(cond_ref, w_ref, b_ref, o_ref):
    cnd = cond_ref[...]
    s = (cnd * jax.nn.sigmoid(cnd)).astype(BF16)
    o_ref[0] = jnp.dot(s, w_ref[0].astype(BF16), preferred_element_type=F32) + b_ref[0]


def _adaln(cond8, w_mod, b_mod):
    depth, d, nm = w_mod.shape
    tn = 1536
    return pl.pallas_call(
        _adaln_kernel,
        grid=(depth, nm // tn),
        in_specs=[pl.BlockSpec((8, d), lambda l, j: (0, 0)),
                  pl.BlockSpec((1, d, tn), lambda l, j: (l, 0, j)),
                  pl.BlockSpec((1, 1, tn), lambda l, j: (l, 0, j))],
        out_specs=pl.BlockSpec((1, 8, tn), lambda l, j: (l, 0, j)),
        out_shape=jax.ShapeDtypeStruct((depth, 8, nm), F32),
        compiler_params=_cparams(("arbitrary", "arbitrary")),
        name="adaln",
    )(cond8, w_mod, b_mod.reshape(depth, 1, nm))


def _rms_tokens(x, gain):
    ms = jnp.mean(x * x, axis=-1, keepdims=True)
    return x * lax.rsqrt(ms + NORM_EPS) * gain


def _rms_rows(x, g):
    ms = jnp.mean(x * x, axis=0, keepdims=True)
    return x * lax.rsqrt(ms + NORM_EPS) * g


def _rope_rows(y, cos, sin):
    r = y.shape[0] // 2
    y1, y2 = y[:r], y[r:]
    return y1 * cos - y2 * sin, y2 * cos + y1 * sin


def _pre_kernel(x_ref, mod_ref, gmix_ref, w_ref, wuq_ref, wukv_ref, gcol_ref, rope_ref,
                qa_ref, ka_ref, va_ref, qb_ref, kb_ref, vb_ref, qc_ref, kc_ref, vc_ref, gates_ref):
    x = x_ref[0]
    sh1 = mod_ref[0, 0, 0:1, :]
    sc1 = mod_ref[0, 0, 1:2, :]
    h = _rms_tokens(x, gmix_ref[...]) * (1.0 + sc1) + sh1
    hb = h.astype(BF16)
    proj = lax.dot_general(w_ref[0:_O_GATE, :], hb, _NT, preferred_element_type=F32)
    gates = lax.dot_general(w_ref[_O_GATE:_O_END, :], hb, _NT, preferred_element_type=F32)
    gates_ref[0, 0] = jax.nn.sigmoid(gates)

    cos64, sin64 = rope_ref[0, 0:32, :], rope_ref[0, 32:64, :]
    cos32, sin32 = rope_ref[0, 64:80, :], rope_ref[0, 80:96, :]

    def gcol(a, b):
        return gcol_ref[a:b, :]

    g_aq, g_ak = gcol(_G_AQ, _G_AK), gcol(_G_AK, _G_BQA)
    a_scale = A_DIM ** -0.5 * LOG2E
    for hd in range(A_HEADS):
        r0 = _O_AQ + hd * A_DIM
        o1, o2 = _rope_rows(_rms_rows(proj[r0:r0 + A_DIM], g_aq), cos64, sin64)
        qa_ref[0, 0, hd * A_DIM:hd * A_DIM + 32, :] = (o1 * a_scale).astype(BF16)
        qa_ref[0, 0, hd * A_DIM + 32:(hd + 1) * A_DIM, :] = (o2 * a_scale).astype(BF16)
    k_parts = []
    for g in range(A_KV_HEADS):
        r0 = _O_AK + g * A_DIM
        o1, o2 = _rope_rows(_rms_rows(proj[r0:r0 + A_DIM], g_ak), cos64, sin64)
        k_parts += [o1, o2]
    ka_ref[0] = jnp.concatenate(k_parts, axis=0).T.astype(BF16)
    va_ref[0, 0] = proj[_O_AV:_O_BCQ].astype(BF16)

    g_bq, g_bk = gcol(_G_BQ, _G_BK), gcol(_G_BK, _G_CQ)
    b_scale = MLA_SCALE * LOG2E
    cqn = _rms_rows(proj[_O_BCQ:_O_BCKV], gcol(_G_BQA, _G_BKVA)).astype(BF16)
    bq = jnp.dot(wuq_ref[...], cqn, preferred_element_type=F32)
    for hd in range(B_HEADS):
        r0 = hd * B_QW
        qn = _rms_rows(bq[r0:r0 + B_NOPE], g_bq[:B_NOPE])
        qr = _rms_rows(bq[r0 + B_NOPE:r0 + B_QW], g_bq[B_NOPE:])
        r1, r2 = _rope_rows(qr, cos32, sin32)
        qb_ref[0, 0, r0:r0 + B_NOPE, :] = (qn * b_scale).astype(BF16)
        qb_ref[0, 0, r0 + B_NOPE:r0 + B_NOPE + 16, :] = (r1 * b_scale).astype(BF16)
        qb_ref[0, 0, r0 + B_NOPE + 16:r0 + B_QW, :] = (r2 * b_scale).astype(BF16)
    ckvn = _rms_rows(proj[_O_BCKV:_O_BKR], gcol(_G_BKVA, _G_BQ)).astype(BF16)
    bkv = jnp.dot(wukv_ref[...], ckvn, preferred_element_type=F32)
    kr = _rms_rows(proj[_O_BKR:_O_CQ], g_bk[B_NOPE:])
    kr1, kr2 = _rope_rows(kr, cos32, sin32)
    zpad = jnp.zeros((LANES - B_QW, kr1.shape[1]), F32)
    k_parts = []
    for hd in range(B_HEADS):
        r0 = hd * (B_NOPE + B_V)
        kn = _rms_rows(bkv[r0:r0 + B_NOPE], g_bk[:B_NOPE])
        k_parts += [kn, kr1, kr2, zpad]
        vb_ref[0, 0, hd * B_V:(hd + 1) * B_V, :] = bkv[r0 + B_NOPE:r0 + B_NOPE + B_V].astype(BF16)
    kb_ref[0] = jnp.concatenate(k_parts, axis=0).T.astype(BF16)

    g_cq, g_ck = gcol(_G_CQ, _G_CK), gcol(_G_CK, _G_END)
    c_scale = C_DIM ** -0.5 * LOG2E
    k_parts = []
    for j in range(2 * C_HEADS):
        r0 = _O_CQ + j * C_DIM
        o1, o2 = _rope_rows(_rms_rows(proj[r0:r0 + C_DIM], g_cq), cos32, sin32)
        qc_ref[0, 0, j * C_DIM:j * C_DIM + 16, :] = (o1 * c_scale).astype(BF16)
        qc_ref[0, 0, j * C_DIM + 16:(j + 1) * C_DIM, :] = (o2 * c_scale).astype(BF16)
        r0 = _O_CK + j * C_DIM
        o1, o2 = _rope_rows(_rms_rows(proj[r0:r0 + C_DIM], g_ck), cos32, sin32)
        k_parts += [o1, o2]
    kc_ref[0] = jnp.concatenate(k_parts, axis=0).T.astype(BF16)
    vc_ref[0, 0] = proj[_O_CV:_O_GATE].astype(BF16)


def _pre_mixer(xs, modtab, gmix, w_int, wuq_t, wukv_t, gcol, rope, n_ctx_tiles):
    bsz, ntok, d = xs.shape
    nt = ntok // TM

    def fm(rows, dtype=BF16):
        return (jax.ShapeDtypeStruct((bsz, nt, rows, TM), dtype),
                pl.BlockSpec((1, 1, rows, TM), lambda b, i: (b, i, 0, 0)))

    def tmaj(cols):
        return (jax.ShapeDtypeStruct((bsz, ntok, cols), BF16),
                pl.BlockSpec((1, TM, cols), lambda b, i: (b, i, 0)))

    outs = [fm(A_HEADS * A_DIM), tmaj(LANES), fm(A_KV_HEADS * A_DIM),
            fm(B_HEADS * B_QW), tmaj(B_HEADS * LANES), fm(B_HEADS * B_V),
            fm(2 * C_HEADS * C_DIM), tmaj(2 * C_HEADS * C_DIM), fm(C_HEADS * C_V),
            fm(3 * D_MODEL, F32)]
    const = lambda b, i: (0, 0)
    return pl.pallas_call(
        _pre_kernel,
        grid=(bsz, nt),
        in_specs=[pl.BlockSpec((1, TM, d), lambda b, i: (b, i, 0)),
                  pl.BlockSpec((1, 1, N_MOD, d), lambda b, i: (b, jnp.where(i < n_ctx_tiles, 0, 1), 0, 0)),
                  pl.BlockSpec((1, d), const),
                  pl.BlockSpec(w_int.shape, const),
                  pl.BlockSpec(wuq_t.shape, const),
                  pl.BlockSpec(wukv_t.shape, const),
                  pl.BlockSpec(gcol.shape, const),
                  pl.BlockSpec((1, rope.shape[1], TM), lambda b, i: (i, 0, 0))],
        out_specs=[o[1] for o in outs],
        out_shape=[o[0] for o in outs],
        compiler_params=_cparams(("arbitrary", "arbitrary")),
        name="pre_mixer",
    )(xs, modtab, gmix, w_int, wuq_t, wukv_t, gcol, rope)


def _attn_kernel(lam_ref, q_ref, k_ref, v_ref, gsub_ref, o_ref, s_scr, *, units, group_size, diff, out_scale,
                 n_ctx_tiles, n_tiles):
    i = pl.program_id(1)
    tq = q_ref.shape[3]
    lat_ks = ATTN_KS if n_tiles % ATTN_KS == 0 else 1

    def padded_q(q_row0, q_rows, pad_off):
        q = q_ref[0, 0, q_row0:q_row0 + q_rows, :]
        parts = []
        if pad_off:
            parts.append(jnp.zeros((pad_off, tq), BF16))
        parts.append(q)
        rest = LANES - pad_off - q_rows
        if rest:
            parts.append(jnp.zeros((rest, tq), BF16))
        return jnp.concatenate(parts, axis=0) if len(parts) > 1 else q

    def run_group(group, ks, n_steps):
        qpads = [padded_q(u[0], u[1], u[2]) for u in group]

        n = len(group)
        pre = min(ATTN_AHEAD, n)
        rows = ks * TM

        def qk_one(j, at_step):
            row0 = pl.multiple_of(at_step * rows, rows)
            k_blk = group[j][3]
            k = k_ref[0, pl.ds(row0, rows), k_blk * LANES:(k_blk + 1) * LANES]
            return jnp.dot(k, qpads[j], preferred_element_type=F32)

        def body(step, state):
            def softmax(j, s):
                m, l, _ = state[j]
                m_new = jnp.maximum(m, jnp.max(s, axis=0, keepdims=True))
                alpha = jnp.exp2(m - m_new)
                p = jnp.exp2(s - m_new)
                l = alpha * l + jnp.sum(p, axis=0, keepdims=True)
                return m_new, l, alpha, p.astype(BF16)

            def pv_one(j, sm):
                m_new, l, alpha, pb = sm
                v_row0 = group[j][4]
                pv = None
                for t in range(ks):
                    v = v_ref[0, step * ks + t, v_row0:v_row0 + HEAD_V, :]
                    d = jnp.dot(v, pb[t * TM:(t + 1) * TM], preferred_element_type=F32)
                    pv = d if pv is None else pv + d
                return m_new, l, alpha * state[j][2] + pv

            nxt = jnp.minimum(step + 1, n_steps - 1)
            scores, sms, new = {}, {}, [None] * n
            for j in range(n):
                s = s_scr[j, 0:rows, :] if j < pre else scores.pop(j)
                sms[j] = softmax(j, s)
                if j + pre < n:
                    scores[j + pre] = qk_one(j + pre, step)
                else:
                    s_scr[j + pre - n, 0:rows, :] = qk_one(j + pre - n, nxt)
                if j - ATTN_LAG >= 0:
                    new[j - ATTN_LAG] = pv_one(j - ATTN_LAG, sms.pop(j - ATTN_LAG))
            for j in sorted(sms):
                new[j] = pv_one(j, sms[j])
            return tuple(new)

        for j in range(pre):
            s_scr[j, 0:rows, :] = qk_one(j, 0)
        init = tuple((jnp.full((1, tq), -1e30, F32), jnp.zeros((1, tq), F32), jnp.zeros((HEAD_V, tq), F32))
                     for _ in group)
        return [acc / l for (_, l, acc) in lax.fori_loop(0, n_steps, body, init)]

    def run_tile(ks, n_steps):
        per_head = 2 if diff else 1
        for g0 in range(0, len(units), group_size):
            outs = run_group(units[g0:g0 + group_size], ks, n_steps)
            for j in range(0, group_size, per_head):
                hd = (g0 + j) // per_head
                if diff:
                    o = outs[j] - lam_ref[0] * outs[j + 1]
                    o = _rms_rows(o, gsub_ref[...]) * out_scale
                else:
                    o = outs[j]
                o_ref[0, 0, hd * HEAD_V:(hd + 1) * HEAD_V, :] = o.astype(BF16)

    @pl.when(i < n_ctx_tiles)
    def _():
        run_tile(1, n_ctx_tiles)

    @pl.when(i >= n_ctx_tiles)
    def _():
        run_tile(lat_ks, n_tiles // lat_ks)


def _attention(lam, qt, k, vt, gsub, *, units, diff, out_scale, n_ctx_tiles, name):
    bsz, nt, qrows, _ = qt.shape
    ntok, kcols = k.shape[1], k.shape[2]
    vrows = vt.shape[2]
    kern = functools.partial(_attn_kernel, units=units, group_size=ATTN_GROUP, diff=diff, out_scale=out_scale,
                             n_ctx_tiles=n_ctx_tiles, n_tiles=nt)
    return pl.pallas_call(
        kern,
        grid=(bsz, nt),
        in_specs=[pl.BlockSpec(memory_space=pltpu.SMEM),
                  pl.BlockSpec((1, 1, qrows, TM), lambda b, i: (b, i, 0, 0)),
                  pl.BlockSpec((1, ntok, kcols), lambda b, i: (b, 0, 0), pipeline_mode=pl.Buffered(1)),
                  pl.BlockSpec((1, nt, vrows, TM), lambda b, i: (b, 0, 0, 0), pipeline_mode=pl.Buffered(1)),
                  pl.BlockSpec(gsub.shape, lambda b, i: (0, 0))],
        out_specs=pl.BlockSpec((1, 1, ATTN_WIDTH, TM), lambda b, i: (b, i, 0, 0)),
        out_shape=jax.ShapeDtypeStruct((bsz, nt, ATTN_WIDTH, TM), BF16),
        scratch_shapes=[pltpu.VMEM((ATTN_AHEAD, ATTN_KS * TM, TM), F32)],
        compiler_params=_cparams(("arbitrary", "arbitrary")),
        name=name,
    )(lam, qt, k, vt, gsub)


_UNITS_A = tuple((h * A_DIM, A_DIM, A_DIM * (h // (A_HEADS // A_KV_HEADS)), 0,
                  HEAD_V * (h // (A_HEADS // A_KV_HEADS))) for h in range(A_HEADS))
_UNITS_B = tuple((h * B_QW, B_QW, 0, h, h * B_V) for h in range(B_HEADS))
_UNITS_C = tuple((h * 2 * C_DIM + c * C_DIM, C_DIM, C_DIM * (2 * (h % 2) + c), h // 2, h * C_V)
                 for h in range(C_HEADS) for c in range(2))


def _merge_kernel(x_ref, mod_ref, oa_ref, ob_ref, oc_ref, gates_ref, woa_ref, wob_ref, woc_ref, wout_ref,
                  gffn_ref, rw_ref, rb_ref, xo_ref, h2_ref, tidx_ref, tw_ref):
    d = D_MODEL
    ya = jnp.dot(woa_ref[...], oa_ref[0, 0], preferred_element_type=F32)
    yb = jnp.dot(wob_ref[...], ob_ref[0, 0], preferred_element_type=F32)
    yc = jnp.dot(woc_ref[...], oc_ref[0, 0], preferred_element_type=F32)
    yt = (gates_ref[0, 0, 0:d, :] * ya + gates_ref[0, 0, d:2 * d, :] * yb
          + gates_ref[0, 0, 2 * d:3 * d, :] * yc)
    out = lax.dot_general(yt.astype(BF16), wout_ref[...], _TN, preferred_element_type=F32)
    g1 = mod_ref[0, 0, 2:3, :]
    x = x_ref[0] + g1 * out
    xo_ref[0] = x
    sh2, sc2 = mod_ref[0, 0, 3:4, :], mod_ref[0, 0, 4:5, :]
    h2 = _rms_tokens(x, gffn_ref[...]) * (1.0 + sc2) + sh2
    h2_ref[0] = h2
    logits = lax.dot_general(rw_ref[...], h2.astype(BF16), _NT, preferred_element_type=F32) + rb_ref[...]
    iota = lax.broadcasted_iota(jnp.int32, logits.shape, 0)
    vals, idxs = [], []
    cur = logits
    for _ in range(TOP_K):
        mx = jnp.max(cur, axis=0, keepdims=True)
        ix = jnp.min(jnp.where(cur == mx, iota, N_EXPERTS), axis=0, keepdims=True)
        vals.append(mx)
        idxs.append(ix)
        cur = jnp.where(iota == ix, -jnp.inf, cur)
    tv = jnp.concatenate(vals, axis=0)
    e = jnp.exp(tv - tv[0:1])
    tw_ref[0, 0] = e / jnp.sum(e, axis=0, keepdims=True)
    tidx_ref[0, 0] = jnp.concatenate(idxs, axis=0)


def _merge(xs, modtab, oa, ob, oc, gates, woa_t, wob_t, woc_t, wout, gffn, rw_t, rb_col, n_ctx_tiles):
    bsz, ntok, d = xs.shape
    nt = ntok // TM
    const = lambda b, i: (0, 0)
    tile4 = lambda b, i: (b, i, 0, 0)
    tok3 = lambda b, i: (b, i, 0)
    return pl.pallas_call(
        _merge_kernel,
        grid=(bsz, nt),
        in_specs=[pl.BlockSpec((1, TM, d), tok3),
                  pl.BlockSpec((1, 1, N_MOD, d), lambda b, i: (b, jnp.where(i < n_ctx_tiles, 0, 1), 0, 0)),
                  pl.BlockSpec((1, 1, ATTN_WIDTH, TM), tile4),
                  pl.BlockSpec((1, 1, ATTN_WIDTH, TM), tile4),
                  pl.BlockSpec((1, 1, ATTN_WIDTH, TM), tile4),
                  pl.BlockSpec((1, 1, 3 * d, TM), tile4),
                  pl.BlockSpec(woa_t.shape, const),
                  pl.BlockSpec(wob_t.shape, const),
                  pl.BlockSpec(woc_t.shape, const),
                  pl.BlockSpec(wout.shape, const),
                  pl.BlockSpec((1, d), const),
                  pl.BlockSpec(rw_t.shape, const),
                  pl.BlockSpec(rb_col.shape, const)],
        out_specs=[pl.BlockSpec((1, TM, d), tok3),
                   pl.BlockSpec((1, TM, d), tok3),
                   pl.BlockSpec((1, 1, TOP_K, TM), tile4),
                   pl.BlockSpec((1, 1, TOP_K, TM), tile4)],
        out_shape=[jax.ShapeDtypeStruct((bsz, ntok, d), F32),
                   jax.ShapeDtypeStruct((bsz, ntok, d), F32),
                   jax.ShapeDtypeStruct((bsz, nt, TOP_K, TM), jnp.int32),
                   jax.ShapeDtypeStruct((bsz, nt, TOP_K, TM), F32)],
        compiler_params=_cparams(("arbitrary", "arbitrary")),
        name="merge_router",
    )(xs, modtab, oa, ob, oc, gates, woa_t, wob_t, woc_t, wout, gffn, rw_t, rb_col)


def _moe_kernel(be_ref, nv_ref, first_ref, nused_ref, dst_prev_ref, src_cur_ref, dst_cur_ref, src_next_ref,
                x_hbm, wgu_ref, bgu_ref, wdn_ref, bdn_ref, y_hbm,
                xbuf0, xbuf1, ybuf0, ybuf1, wgu_bf, wdn_bf, gsem, ssem, *, spare_first):
    i = pl.program_id(0)
    n_used = nused_ref[0]
    d = D_MODEL

    def gather_row(src_ref, r, xb, sem):
        return pltpu.make_async_copy(x_hbm.at[pl.ds(src_ref[0, 0, r], 1), :], xb.at[pl.ds(r, 1), :], sem)

    def scatter_row(row, r, yb, sem):
        return pltpu.make_async_copy(yb.at[pl.ds(r, 1), :], y_hbm.at[pl.ds(row, 1), :], sem)

    def wait_gather(xb, sem):
        pltpu.make_async_copy(x_hbm.at[pl.ds(0, MOE_TB), :], xb, sem).wait()

    def wait_scatter(yb, sem):
        pltpu.make_async_copy(yb, y_hbm.at[pl.ds(0, MOE_TB), :], sem).wait()

    def step(p):
        xb_cur, xb_nxt = (xbuf0, xbuf1) if p == 0 else (xbuf1, xbuf0)
        yb_cur, yb_prv = (ybuf0, ybuf1) if p == 0 else (ybuf1, ybuf0)
        q = 1 - p

        if p == 0:
            @pl.when(i == 0)
            def _():
                for r in range(MOE_TB):
                    gather_row(src_cur_ref, r, xb_cur, gsem.at[p]).start()
                yb_prv[...] = jnp.zeros_like(yb_prv)
                spare = [pltpu.make_async_copy(
                    yb_prv, y_hbm.at[pl.ds(spare_first - e * MOE_TB, MOE_TB), :], ssem.at[q])
                    for e in range(N_EXPERTS + 1)]
                for cp in spare:
                    cp.start()
                for cp in spare:
                    cp.wait()

        @pl.when(first_ref[i] == 1)
        def _():
            wgu_bf[...] = wgu_ref[0, 0].astype(BF16)
            wdn_bf[...] = wdn_ref[0, 0].astype(BF16)

        wait_gather(xb_cur, gsem.at[p])

        @pl.when(i >= 1)
        def _():
            wait_scatter(yb_cur, ssem.at[p])

        has_prev = i >= 1
        xb = xb_cur[...].astype(BF16)
        n_chunks = d // MOE_CW
        rows_per_chunk = MOE_TB // n_chunks
        y = None
        for c in range(n_chunks):
            for r in range(c * rows_per_chunk, (c + 1) * rows_per_chunk):
                gather_row(src_next_ref, r, xb_nxt, gsem.at[q]).start()
                row = jnp.where(has_prev, dst_prev_ref[0, 0, r], spare_first + r)
                scatter_row(row, r, yb_prv, ssem.at[q]).start()
            c0, c1 = c * MOE_CW, (c + 1) * MOE_CW
            glu = jnp.dot(xb, wgu_bf[:, c0:c1], preferred_element_type=F32) + bgu_ref[0, 0, :, c0:c1]
            lin = jnp.dot(xb, wgu_bf[:, d + c0:d + c1], preferred_element_type=F32) + bgu_ref[0, 0, :, d + c0:d + c1]
            glu = jnp.minimum(glu, SWIGLU_LIMIT)
            lin = jnp.clip(lin, -SWIGLU_LIMIT, SWIGLU_LIMIT)
            act = glu * jax.nn.sigmoid(SWIGLU_ALPHA * glu) * (lin + 1.0)
            part = jnp.dot(act.astype(BF16), wdn_bf[c0:c1, :], preferred_element_type=F32)
            y = part if y is None else y + part
        yb_cur[...] = y + bdn_ref[0, 0]

        @pl.when(i == n_used - 1)
        def _():
            for r in range(MOE_TB):
                scatter_row(dst_cur_ref[0, 0, r], r, yb_cur, ssem.at[p]).start()
            wait_scatter(yb_prv, ssem.at[q])
            wait_scatter(yb_cur, ssem.at[p])
            wait_gather(xb_nxt, gsem.at[q])

    for p in range(2):
        @pl.when((i < n_used) & (i % 2 == p))
        def _(p=p):
            step(p)


def _moe_experts(h2_flat, plan, w_gu, b_gu, w_dn, b_dn, layer):
    block_e, block_nv, block_first, n_used, slot_src, slot_dst = plan
    n_blocks = block_e.shape[0]
    n_tok, d = h2_flat.shape
    idx_block = lambda f: pl.BlockSpec((1, 1, MOE_TB), lambda i, be, nv, fs, nu: (f(i), 0, 0),
                                       memory_space=pltpu.SMEM)
    wmap = lambda i, be, nv, fs, nu: (layer, be[i], 0, 0)
    grid_spec = pltpu.PrefetchScalarGridSpec(
        num_scalar_prefetch=4,
        grid=(n_blocks,),
        in_specs=[idx_block(lambda i: jnp.maximum(i - 1, 0)),
                  idx_block(lambda i: i),
                  idx_block(lambda i: i),
                  idx_block(lambda i: jnp.minimum(i + 1, n_blocks - 1)),
                  pl.BlockSpec(memory_space=pl.ANY),
                  pl.BlockSpec((1, 1, d, 2 * d), wmap),
                  pl.BlockSpec((1, 1, 1, 2 * d), wmap),
                  pl.BlockSpec((1, 1, d, d), wmap),
                  pl.BlockSpec((1, 1, 1, d), wmap)],
        out_specs=pl.BlockSpec(memory_space=pl.ANY),
        scratch_shapes=[pltpu.VMEM((MOE_TB, d), F32), pltpu.VMEM((MOE_TB, d), F32),
                        pltpu.VMEM((MOE_TB, d), F32), pltpu.VMEM((MOE_TB, d), F32),
                        pltpu.VMEM((d, 2 * d), BF16), pltpu.VMEM((d, d), BF16),
                        pltpu.SemaphoreType.DMA((2,)), pltpu.SemaphoreType.DMA((2,))],
    )
    depth, ne = b_gu.shape[0], b_gu.shape[1]
    spare_first = _moe_out_rows(n_tok) - MOE_TB
    src3 = slot_src.reshape(n_blocks, 1, MOE_TB)
    dst3 = slot_dst.reshape(n_blocks, 1, MOE_TB)
    return pl.pallas_call(
        functools.partial(_moe_kernel, spare_first=spare_first),
        grid_spec=grid_spec,
        out_shape=jax.ShapeDtypeStruct((_moe_out_rows(n_tok), d), F32),
        compiler_params=pltpu.CompilerParams(dimension_semantics=("arbitrary",), vmem_limit_bytes=VMEM_LIMIT,
                                             has_side_effects=True),
        name="moe_experts",
    )(block_e, block_nv, block_first, n_used, dst3, src3, dst3, src3, h2_flat,
      w_gu, b_gu.reshape(depth, ne, 1, 2 * d), w_dn, b_dn.reshape(depth, ne, 1, d))


def _moe_out_rows(n_tok):
    return TOP_K * n_tok + (N_EXPERTS + 1) * MOE_TB


def _moe_plan(top_idx):
    flat_e = top_idx.reshape(-1)
    n_assign = flat_e.shape[0]
    order = jnp.argsort(flat_e, stable=True).astype(jnp.int32)
    counts = jnp.sum((flat_e[None, :] == jnp.arange(N_EXPERTS, dtype=jnp.int32)[:, None]).astype(jnp.int32), axis=1)
    starts = jnp.cumsum(counts) - counts
    nblk = (counts + MOE_TB - 1) // MOE_TB
    blk_ends = jnp.cumsum(nblk)
    blk_starts = blk_ends - nblk
    n_blocks = (n_assign + N_EXPERTS * (MOE_TB - 1) + MOE_TB - 1) // MOE_TB
    bi = jnp.arange(n_blocks, dtype=jnp.int32)
    block_e = jnp.minimum(jnp.sum((bi[:, None] >= blk_ends[None, :]).astype(jnp.int32), axis=1), N_EXPERTS - 1)
    within = bi - blk_starts[block_e]
    n_used = blk_ends[N_EXPERTS - 1]
    used = bi < n_used
    block_nv = jnp.where(used, jnp.clip(counts[block_e] - within * MOE_TB, 0, MOE_TB), 0).astype(jnp.int32)
    block_first = (used & (within == 0)).astype(jnp.int32)
    off = jnp.where(used, starts[block_e] + within * MOE_TB, 0)
    order_pad = jnp.concatenate([order, jnp.zeros((MOE_TB,), jnp.int32)])
    slot_assign = jax.vmap(lambda o: lax.dynamic_slice(order_pad, (o,), (MOE_TB,)))(off)
    n_tok = n_assign // TOP_K
    k = lax.shift_right_logical(slot_assign, _TM_SHIFT) & (TOP_K - 1)
    tok = (lax.shift_left(lax.shift_right_logical(slot_assign, _TM_SHIFT + _K_SHIFT), _TM_SHIFT)
           | (slot_assign & (TM - 1)))
    r = jnp.arange(MOE_TB, dtype=jnp.int32)[None, :]
    spare = TOP_K * n_tok + block_e[:, None] * MOE_TB + r
    slot_dst = jnp.where(r < block_nv[:, None], k * n_tok + tok, spare).astype(jnp.int32)
    return (block_e.astype(jnp.int32), block_nv, block_first, n_used.reshape(1).astype(jnp.int32),
            tok.astype(jnp.int32), slot_dst)


def _combine_kernel(x_ref, mod_ref, tw_ref, y0_ref, y1_ref, y2_ref, y3_ref, o_ref):
    tm = x_ref.shape[1]
    w8 = jnp.concatenate([tw_ref[0, 0], jnp.zeros((8 - TOP_K, tm), F32)], axis=0)
    wt = w8.T
    tot = (wt[:, 0:1] * y0_ref[...] + wt[:, 1:2] * y1_ref[...]) + (wt[:, 2:3] * y2_ref[...] + wt[:, 3:4] * y3_ref[...])
    o_ref[0] = x_ref[0] + mod_ref[0, 0, 5:6, :] * tot


def _combine(xs, modtab, top_w, y, n_ctx_tiles, skip_tiles):
    bsz, ntok, d = xs.shape
    nt = ntok // TM
    n_out = nt - skip_tiles
    yrow = lambda k: pl.BlockSpec((TM, d), lambda b, i: ((k * bsz + b) * nt + i + skip_tiles, 0))
    return pl.pallas_call(
        _combine_kernel,
        grid=(bsz, n_out),
        in_specs=[pl.BlockSpec((1, TM, d), lambda b, i: (b, i + skip_tiles, 0)),
                  pl.BlockSpec((1, 1, N_MOD, d),
                               lambda b, i: (b, jnp.where(i + skip_tiles < n_ctx_tiles, 0, 1), 0, 0)),
                  pl.BlockSpec((1, 1, TOP_K, TM), lambda b, i: (b, i + skip_tiles, 0, 0)),
                  yrow(0), yrow(1), yrow(2), yrow(3)],
        out_specs=pl.BlockSpec((1, TM, d), lambda b, i: (b, i, 0)),
        out_shape=jax.ShapeDtypeStruct((bsz, n_out * TM, d), F32),
        compiler_params=_cparams(("arbitrary", "arbitrary")),
        name="moe_combine",
    )(xs, modtab, top_w, y, y, y, y)


def _rope_table(n_ctx, n_lat):
    n_rows = n_lat // GRID_W
    row = jnp.repeat(jnp.arange(n_rows, dtype=F32), GRID_W)
    col = jnp.tile(jnp.arange(GRID_W, dtype=F32), n_rows)

    def table(rot_dim):
        axis_pairs = rot_dim // 4
        inv_freq = ROPE_THETA ** (-jnp.arange(axis_pairs, dtype=F32) / axis_pairs)
        ang = jnp.concatenate([row[:, None] * inv_freq, col[:, None] * inv_freq], axis=-1)
        cos = jnp.concatenate([jnp.ones((n_ctx, 2 * axis_pairs), F32), jnp.cos(ang)], axis=0)
        sin = jnp.concatenate([jnp.zeros((n_ctx, 2 * axis_pairs), F32), jnp.sin(ang)], axis=0)
        return cos, sin

    c64, s64 = table(A_DIM)
    c32, s32 = table(B_ROPE)
    tab = jnp.concatenate([c64, s64, c32, s32], axis=-1)
    ntok = n_ctx + n_lat
    return jnp.transpose(tab.reshape(ntok // TM, TM, tab.shape[1]), (0, 2, 1))


def _cols(v):
    return jnp.broadcast_to(v.astype(F32)[:, None], (v.shape[0], TM))


def kernel(x, c, ctx, c_ctx, w_mod, b_mod, norm_mix, norm_ffn, w_in, a_q_norm, a_k_norm, b_q_a_norm,
           b_kv_a_norm, b_w_uq, b_w_ukv, b_q_norm, b_k_norm, c_q_norm, c_k_norm, c_lambda, c_subln,
           w_o_a, w_o_b, w_o_c, w_out, router_w, router_b, exp_w_gu, exp_b_gu, exp_w_down, exp_b_down):
    bsz, n_lat, d = x.shape
    n_ctx = ctx.shape[1]
    depth = w_mod.shape[0]
    assert d == D_MODEL and n_ctx % TM == 0 and n_lat % TM == 0 and n_lat % GRID_W == 0 and bsz + 1 <= 8
    n_ctx_tiles = n_ctx // TM
    ntok = n_ctx + n_lat

    xs = jnp.concatenate([ctx, x], axis=1)
    cond8 = jnp.zeros((8, d), F32).at[:bsz].set(c).at[bsz].set(c_ctx)
    mod_all = _adaln(cond8, w_mod, b_mod).reshape(depth, 8, N_MOD, d)
    rope = _rope_table(n_ctx, n_lat)

    for l in range(depth):
        lam_init = 0.8 - 0.6 * math.exp(-0.3 * l)
        lq1, lk1, lq2, lk2 = c_lambda[l].astype(F32)
        lam = (jnp.exp(jnp.sum(lq1 * lk1)) - jnp.exp(jnp.sum(lq2 * lk2)) + lam_init).reshape(1)
        modtab = jnp.stack([jnp.broadcast_to(mod_all[l, bsz], (bsz, N_MOD, d)), mod_all[l, :bsz]], axis=1)
        gcol = jnp.concatenate([_cols(a_q_norm[l]), _cols(a_k_norm[l]), _cols(b_q_a_norm[l]),
                                _cols(b_kv_a_norm[l]), _cols(b_q_norm[l]), _cols(b_k_norm[l]),
                                _cols(c_q_norm[l]), _cols(c_k_norm[l])], axis=0)
        (qa, ka, va, qb, kb, vb, qc, kc, vc, gates) = _pre_mixer(
            xs, modtab, norm_mix[l].reshape(1, d), w_in[l].T.astype(BF16), b_w_uq[l].T.astype(BF16),
            b_w_ukv[l].T.astype(BF16), gcol, rope, n_ctx_tiles)

        gsub = _cols(c_subln[l])
        attn = functools.partial(_attention, lam, gsub=gsub, n_ctx_tiles=n_ctx_tiles)
        oa = attn(qa, ka, va, units=_UNITS_A, diff=False, out_scale=1.0, name="attn_gqa")
        ob = attn(qb, kb, vb, units=_UNITS_B, diff=False, out_scale=1.0, name="attn_mla")
        oc = attn(qc, kc, vc, units=_UNITS_C, diff=True, out_scale=1.0 - lam_init, name="attn_diff")

        xs, h2, top_idx, top_w = _merge(
            xs, modtab, oa, ob, oc, gates, w_o_a[l].T.astype(BF16), w_o_b[l].T.astype(BF16),
            w_o_c[l].T.astype(BF16), w_out[l].astype(BF16), norm_ffn[l].reshape(1, d),
            router_w[l].T.astype(BF16), _cols(router_b[l]), n_ctx_tiles)

        y = _moe_experts(h2.reshape(bsz * ntok, d), _moe_plan(top_idx), exp_w_gu, exp_b_gu, exp_w_down,
                         exp_b_down, l)
        last = l == depth - 1
        xs = _combine(xs, modtab, top_w, y, n_ctx_tiles, n_ctx_tiles if last else 0)
    return xs
```

```python
import functools
import math

import numpy as np
import jax
import jax.numpy as jnp
from jax import lax
from jax.experimental import pallas as pl
from jax.experimental.pallas import tpu as pltpu

F32 = jnp.float32
BF16 = jnp.bfloat16

D_MODEL = 1024
GRID_W = 64
ROPE_THETA = 10000.0
NORM_EPS = 1e-6
N_MOD = 6
A_HEADS, A_KV_HEADS, A_DIM = 8, 2, 64
B_HEADS, B_Q_RANK, B_KV_RANK, B_NOPE, B_ROPE, B_V = 8, 256, 128, 64, 32, 64
MLA_SCALE = (B_NOPE + B_ROPE) ** -0.5
LOG2E = math.log2(math.e)
C_HEADS, C_DIM, C_V = 8, 32, 64
N_EXPERTS, TOP_K = 32, 4
SWIGLU_ALPHA, SWIGLU_LIMIT = 1.702, 7.0
HEAD_V = 64
ATTN_WIDTH = 512

_IN_SIZES = (A_HEADS * A_DIM, A_KV_HEADS * A_DIM, A_KV_HEADS * A_DIM, B_Q_RANK, B_KV_RANK, B_ROPE,
             C_HEADS * 2 * C_DIM, C_HEADS * 2 * C_DIM, C_HEADS * C_V, 3 * D_MODEL)
_OFF = tuple(int(v) for v in np.cumsum((0,) + _IN_SIZES))
(_O_AQ, _O_AK, _O_AV, _O_BCQ, _O_BCKV, _O_BKR, _O_CQ, _O_CK, _O_CV, _O_GATE, _O_END) = _OFF
B_QW = B_NOPE + B_ROPE

_G_SIZES = (A_DIM, A_DIM, B_Q_RANK, B_KV_RANK, B_QW, B_QW, C_DIM, C_DIM)
_GOFF = tuple(int(v) for v in np.cumsum((0,) + _G_SIZES))
(_G_AQ, _G_AK, _G_BQA, _G_BKVA, _G_BQ, _G_BK, _G_CQ, _G_CK, _G_END) = _GOFF

LANES = 128
TM = 256
MOE_TB = 256
MOE_CW = 256
_TM_SHIFT = TM.bit_length() - 1
_K_SHIFT = TOP_K.bit_length() - 1
assert TM == 1 << _TM_SHIFT and TOP_K == 1 << _K_SHIFT
ATTN_GROUP = 16
ATTN_AHEAD = 2
ATTN_LAG = 1
ATTN_KS = 3
VMEM_LIMIT = 56 * 1024 * 1024

_NT = (((1,), (1,)), ((), ()))
_TN = (((0,), (0,)), ((), ()))


def _cparams(sem):
    return pltpu.CompilerParams(dimension_semantics=sem, vmem_limit_bytes=VMEM_LIMIT)


def _adaln_kernel(cond_ref, w_ref, b_ref, o_ref):
    cnd = cond_ref[...]
    s = (cnd * jax.nn.sigmoid(cnd)).astype(BF16)
    o_ref[0] = jnp.dot(s, w_ref[0].astype(BF16), preferred_element_type=F32) + b_ref[0]


def _adaln(cond8, w_mod, b_mod):
    depth, d, nm = w_mod.shape
    tn = 1536
    return pl.pallas_call(
        _adaln_kernel,
        grid=(depth, nm // tn),
        in_specs=[pl.BlockSpec((8, d), lambda l, j: (0, 0)),
                  pl.BlockSpec((1, d, tn), lambda l, j: (l, 0, j)),
                  pl.BlockSpec((1, 1, tn), lambda l, j: (l, 0, j))],
        out_specs=pl.BlockSpec((1, 8, tn), lambda l, j: (l, 0, j)),
        out_shape=jax.ShapeDtypeStruct((depth, 8, nm), F32),
        compiler_params=_cparams(("arbitrary", "arbitrary")),
        name="adaln",
    )(cond8, w_mod, b_mod.reshape(depth, 1, nm))


def _rms_tokens(x, gain):
    ms = jnp.mean(x * x, axis=-1, keepdims=True)
    return x * lax.rsqrt(ms + NORM_EPS) * gain


def _rms_rows(x, g):
    ms = jnp.mean(x * x, axis=0, keepdims=True)
    return x * lax.rsqrt(ms + NORM_EPS) * g


def _rope_rows(y, cos, sin):
    r = y.shape[0] // 2
    y1, y2 = y[:r], y[r:]
    return y1 * cos - y2 * sin, y2 * cos + y1 * sin


def _pre_kernel(x_ref, mod_ref, gmix_ref, w_ref, wuq_ref, wukv_ref, gcol_ref, rope_ref,
                qa_ref, ka_ref, va_ref, qb_ref, kb_ref, vb_ref, qc_ref, kc_ref, vc_ref, gates_ref):
    x = x_ref[0]
    sh1 = mod_ref[0, 0, 0:1, :]
    sc1 = mod_ref[0, 0, 1:2, :]
    h = _rms_tokens(x, gmix_ref[...]) * (1.0 + sc1) + sh1
    hb = h.astype(BF16)
    proj = lax.dot_general(w_ref[0:_O_GATE, :], hb, _NT, preferred_element_type=F32)
    gates = lax.dot_general(w_ref[_O_GATE:_O_END, :], hb, _NT, preferred_element_type=F32)
    gates_ref[0, 0] = jax.nn.sigmoid(gates)

    cos64, sin64 = rope_ref[0, 0:32, :], rope_ref[0, 32:64, :]
    cos32, sin32 = rope_ref[0, 64:80, :], rope_ref[0, 80:96, :]

    def gcol(a, b):
        return gcol_ref[a:b, :]

    g_aq, g_ak = gcol(_G_AQ, _G_AK), gcol(_G_AK, _G_BQA)
    a_scale = A_DIM ** -0.5 * LOG2E
    for hd in range(A_HEADS):
        r0 = _O_AQ + hd * A_DIM
        o1, o2 = _rope_rows(_rms_rows(proj[r0:r0 + A_DIM], g_aq), cos64, sin64)
        qa_ref[0, 0, hd * A_DIM:hd * A_DIM + 32, :] = (o1 * a_scale).astype(BF16)
        qa_ref[0, 0, hd * A_DIM + 32:(hd + 1) * A_DIM, :] = (o2 * a_scale).astype(BF16)
    k_parts = []
    for g in range(A_KV_HEADS):
        r0 = _O_AK + g * A_DIM
        o1, o2 = _rope_rows(_rms_rows(proj[r0:r0 + A_DIM], g_ak), cos64, sin64)
        k_parts += [o1, o2]
    ka_ref[0] = jnp.concatenate(k_parts, axis=0).T.astype(BF16)
    va_ref[0, 0] = proj[_O_AV:_O_BCQ].astype(BF16)

    g_bq, g_bk = gcol(_G_BQ, _G_BK), gcol(_G_BK, _G_CQ)
    b_scale = MLA_SCALE * LOG2E
    cqn = _rms_rows(proj[_O_BCQ:_O_BCKV], gcol(_G_BQA, _G_BKVA)).astype(BF16)
    bq = jnp.dot(wuq_ref[...], cqn, preferred_element_type=F32)
    for hd in range(B_HEADS):
        r0 = hd * B_QW
        qn = _rms_rows(bq[r0:r0 + B_NOPE], g_bq[:B_NOPE])
        qr = _rms_rows(bq[r0 + B_NOPE:r0 + B_QW], g_bq[B_NOPE:])
        r1, r2 = _rope_rows(qr, cos32, sin32)
        qb_ref[0, 0, r0:r0 + B_NOPE, :] = (qn * b_scale).astype(BF16)
        qb_ref[0, 0, r0 + B_NOPE:r0 + B_NOPE + 16, :] = (r1 * b_scale).astype(BF16)
        qb_ref[0, 0, r0 + B_NOPE + 16:r0 + B_QW, :] = (r2 * b_scale).astype(BF16)
    ckvn = _rms_rows(proj[_O_BCKV:_O_BKR], gcol(_G_BKVA, _G_BQ)).astype(BF16)
    bkv = jnp.dot(wukv_ref[...], ckvn, preferred_element_type=F32)
    kr = _rms_rows(proj[_O_BKR:_O_CQ], g_bk[B_NOPE:])
    kr1, kr2 = _rope_rows(kr, cos32, sin32)
    zpad = jnp.zeros((LANES - B_QW, kr1.shape[1]), F32)
    k_parts = []
    for hd in range(B_HEADS):
        r0 = hd * (B_NOPE + B_V)
        kn = _rms_rows(bkv[r0:r0 + B_NOPE], g_bk[:B_NOPE])
        k_parts += [kn, kr1, kr2, zpad]
        vb_ref[0, 0, hd * B_V:(hd + 1) * B_V, :] = bkv[r0 + B_NOPE:r0 + B_NOPE + B_V].astype(BF16)
    kb_ref[0] = jnp.concatenate(k_parts, axis=0).T.astype(BF16)

    g_cq, g_ck = gcol(_G_CQ, _G_CK), gcol(_G_CK, _G_END)
    c_scale = C_DIM ** -0.5 * LOG2E
    k_parts = []
    for j in range(2 * C_HEADS):
        r0 = _O_CQ + j * C_DIM
        o1, o2 = _rope_rows(_rms_rows(proj[r0:r0 + C_DIM], g_cq), cos32, sin32)
        qc_ref[0, 0, j * C_DIM:j * C_DIM + 16, :] = (o1 * c_scale).astype(BF16)
        qc_ref[0, 0, j * C_DIM + 16:(j + 1) * C_DIM, :] = (o2 * c_scale).astype(BF16)
        r0 = _O_CK + j * C_DIM
        o1, o2 = _rope_rows(_rms_rows(proj[r0:r0 + C_DIM], g_ck), cos32, sin32)
        k_parts += [o1, o2]
    kc_ref[0] = jnp.concatenate(k_parts, axis=0).T.astype(BF16)
    vc_ref[0, 0] = proj[_O_CV:_O_GATE].astype(BF16)


def _pre_mixer(xs, modtab, gmix, w_int, wuq_t, wukv_t, gcol, rope, n_ctx_tiles):
    bsz, ntok, d = xs.shape
    nt = ntok // TM

    def fm(rows, dtype=BF16):
        return (jax.ShapeDtypeStruct((bsz, nt, rows, TM), dtype),
                pl.BlockSpec((1, 1, rows, TM), lambda b, i: (b, i, 0, 0)))

    def tmaj(cols):
        return (jax.ShapeDtypeStruct((bsz, ntok, cols), BF16),
                pl.BlockSpec((1, TM, cols), lambda b, i: (b, i, 0)))

    outs = [fm(A_HEADS * A_DIM), tmaj(LANES), fm(A_KV_HEADS * A_DIM),
            fm(B_HEADS * B_QW), tmaj(B_HEADS * LANES), fm(B_HEADS * B_V),
            fm(2 * C_HEADS * C_DIM), tmaj(2 * C_HEADS * C_DIM), fm(C_HEADS * C_V),
            fm(3 * D_MODEL, F32)]
    const = lambda b, i: (0, 0)
    return pl.pallas_call(
        _pre_kernel,
        grid=(bsz, nt),
        in_specs=[pl.BlockSpec((1, TM, d), lambda b, i: (b, i, 0)),
                  pl.BlockSpec((1, 1, N_MOD, d), lambda b, i: (b, jnp.where(i < n_ctx_tiles, 0, 1), 0, 0)),
                  pl.BlockSpec((1, d), const),
                  pl.BlockSpec(w_int.shape, const),
                  pl.BlockSpec(wuq_t.shape, const),
                  pl.BlockSpec(wukv_t.shape, const),
                  pl.BlockSpec(gcol.shape, const),
                  pl.BlockSpec((1, rope.shape[1], TM), lambda b, i: (i, 0, 0))],
        out_specs=[o[1] for o in outs],
        out_shape=[o[0] for o in outs],
        compiler_params=_cparams(("arbitrary", "arbitrary")),
        name="pre_mixer",
    )(xs, modtab, gmix, w_int, wuq_t, wukv_t, gcol, rope)


def _attn_kernel(lam_ref, q_ref, k_ref, v_ref, gsub_ref, o_ref, s_scr, *, units, group_size, diff, out_scale,
                 n_ctx_tiles, n_tiles):
    i = pl.program_id(1)
    tq = q_ref.shape[3]
    lat_ks = ATTN_KS if n_tiles % ATTN_KS == 0 else 1

    def padded_q(q_row0, q_rows, pad_off):
        q = q_ref[0, 0, q_row0:q_row0 + q_rows, :]
        parts = []
        if pad_off:
            parts.append(jnp.zeros((pad_off, tq), BF16))
        parts.append(q)
        rest = LANES - pad_off - q_rows
        if rest:
            parts.append(jnp.zeros((rest, tq), BF16))
        return jnp.concatenate(parts, axis=0) if len(parts) > 1 else q

    def run_group(group, ks, n_steps):
        qpads = [padded_q(u[0], u[1], u[2]) for u in group]

        n = len(group)
        pre = min(ATTN_AHEAD, n)
        rows = ks * TM

        def qk_one(j, at_step):
            row0 = pl.multiple_of(at_step * rows, rows)
            k_blk = group[j][3]
            k = k_ref[0, pl.ds(row0, rows), k_blk * LANES:(k_blk + 1) * LANES]
            return jnp.dot(k, qpads[j], preferred_element_type=F32)

        def body(step, state):
            def softmax(j, s):
                m, l, _ = state[j]
                m_new = jnp.maximum(m, jnp.max(s, axis=0, keepdims=True))
                alpha = jnp.exp2(m - m_new)
                p = jnp.exp2(s - m_new)
                l = alpha * l + jnp.sum(p, axis=0, keepdims=True)
                return m_new, l, alpha, p.astype(BF16)

            def pv_one(j, sm):
                m_new, l, alpha, pb = sm
                v_row0 = group[j][4]
                pv = None
                for t in range(ks):
                    v = v_ref[0, step * ks + t, v_row0:v_row0 + HEAD_V, :]
                    d = jnp.dot(v, pb[t * TM:(t + 1) * TM], preferred_element_type=F32)
                    pv = d if pv is None else pv + d
                return m_new, l, alpha * state[j][2] + pv

            nxt = jnp.minimum(step + 1, n_steps - 1)
            scores, sms, new = {}, {}, [None] * n
            for j in range(n):
                s = s_scr[j, 0:rows, :] if j < pre else scores.pop(j)
                sms[j] = softmax(j, s)
                if j + pre < n:
                    scores[j + pre] = qk_one(j + pre, step)
                else:
                    s_scr[j + pre - n, 0:rows, :] = qk_one(j + pre - n, nxt)
                if j - ATTN_LAG >= 0:
                    new[j - ATTN_LAG] = pv_one(j - ATTN_LAG, sms.pop(j - ATTN_LAG))
            for j in sorted(sms):
                new[j] = pv_one(j, sms[j])
            return tuple(new)

        for j in range(pre):
            s_scr[j, 0:rows, :] = qk_one(j, 0)
        init = tuple((jnp.full((1, tq), -1e30, F32), jnp.zeros((1, tq), F32), jnp.zeros((HEAD_V, tq), F32))
                     for _ in group)
        return [acc / l for (_, l, acc) in lax.fori_loop(0, n_steps, body, init)]

    def run_tile(ks, n_steps):
        per_head = 2 if diff else 1
        for g0 in range(0, len(units), group_size):
            group = units[g0:g0 + group_size]
            outs = run_group(group, ks, n_steps)
            for j in range(0, len(group), per_head):
                hd = (g0 + j) // per_head
                if diff:
                    o = outs[j] - lam_ref[0] * outs[j + 1]
                    o = _rms_rows(o, gsub_ref[...]) * out_scale
                else:
                    o = outs[j]
                o_ref[0, 0, hd * HEAD_V:(hd + 1) * HEAD_V, :] = o.astype(BF16)

    @pl.when(i < n_ctx_tiles)
    def _():
        run_tile(1, n_ctx_tiles)

    @pl.when(i >= n_ctx_tiles)
    def _():
        run_tile(lat_ks, n_tiles // lat_ks)


def _attention(lam, qt, k, vt, gsub, *, units, diff, out_scale, n_ctx_tiles, name):
    bsz, nt, qrows, _ = qt.shape
    ntok, kcols = k.shape[1], k.shape[2]
    vrows = vt.shape[2]
    kern = functools.partial(_attn_kernel, units=units, group_size=ATTN_GROUP, diff=diff, out_scale=out_scale,
                             n_ctx_tiles=n_ctx_tiles, n_tiles=nt)
    return pl.pallas_call(
        kern,
        grid=(bsz, nt),
        in_specs=[pl.BlockSpec(memory_space=pltpu.SMEM),
                  pl.BlockSpec((1, 1, qrows, TM), lambda b, i: (b, i, 0, 0)),
                  pl.BlockSpec((1, ntok, kcols), lambda b, i: (b, 0, 0), pipeline_mode=pl.Buffered(1)),
                  pl.BlockSpec((1, nt, vrows, TM), lambda b, i: (b, 0, 0, 0), pipeline_mode=pl.Buffered(1)),
                  pl.BlockSpec(gsub.shape, lambda b, i: (0, 0))],
        out_specs=pl.BlockSpec((1, 1, ATTN_WIDTH, TM), lambda b, i: (b, i, 0, 0)),
        out_shape=jax.ShapeDtypeStruct((bsz, nt, ATTN_WIDTH, TM), BF16),
        scratch_shapes=[pltpu.VMEM((ATTN_AHEAD, ATTN_KS * TM, TM), F32)],
        compiler_params=_cparams(("arbitrary", "arbitrary")),
        name=name,
    )(lam, qt, k, vt, gsub)


_UNITS_A = tuple((h * A_DIM, A_DIM, A_DIM * (h // (A_HEADS // A_KV_HEADS)), 0,
                  HEAD_V * (h // (A_HEADS // A_KV_HEADS))) for h in range(A_HEADS))
_UNITS_B = tuple((h * B_QW, B_QW, 0, h, h * B_V) for h in range(B_HEADS))
_UNITS_C = tuple((h * 2 * C_DIM + c * C_DIM, C_DIM, C_DIM * (2 * (h % 2) + c), h // 2, h * C_V)
                 for h in range(C_HEADS) for c in range(2))


def _merge_kernel(x_ref, mod_ref, oa_ref, ob_ref, oc_ref, gates_ref, woa_ref, wob_ref, woc_ref, wout_ref,
                  gffn_ref, rw_ref, rb_ref, xo_ref, h2_ref, tidx_ref, tw_ref):
    d = D_MODEL
    ya = jnp.dot(woa_ref[...], oa_ref[0, 0], preferred_element_type=F32)
    yb = jnp.dot(wob_ref[...], ob_ref[0, 0], preferred_element_type=F32)
    yc = jnp.dot(woc_ref[...], oc_ref[0, 0], preferred_element_type=F32)
    yt = (gates_ref[0, 0, 0:d, :] * ya + gates_ref[0, 0, d:2 * d, :] * yb
          + gates_ref[0, 0, 2 * d:3 * d, :] * yc)
    out = lax.dot_general(yt.astype(BF16), wout_ref[...], _TN, preferred_element_type=F32)
    g1 = mod_ref[0, 0, 2:3, :]
    x = x_ref[0] + g1 * out
    xo_ref[0] = x
    sh2, sc2 = mod_ref[0, 0, 3:4, :], mod_ref[0, 0, 4:5, :]
    h2 = _rms_tokens(x, gffn_ref[...]) * (1.0 + sc2) + sh2
    h2_ref[0] = h2
    logits = lax.dot_general(rw_ref[...], h2.astype(BF16), _NT, preferred_element_type=F32) + rb_ref[...]
    iota = lax.broadcasted_iota(jnp.int32, logits.shape, 0)
    vals, idxs = [], []
    cur = logits
    for _ in range(TOP_K):
        mx = jnp.max(cur, axis=0, keepdims=True)
        ix = jnp.min(jnp.where(cur == mx, iota, N_EXPERTS), axis=0, keepdims=True)
        vals.append(mx)
        idxs.append(ix)
        cur = jnp.where(iota == ix, -jnp.inf, cur)
    tv = jnp.concatenate(vals, axis=0)
    e = jnp.exp(tv - tv[0:1])
    tw_ref[0, 0] = e / jnp.sum(e, axis=0, keepdims=True)
    tidx_ref[0, 0] = jnp.concatenate(idxs, axis=0)


def _merge(xs, modtab, oa, ob, oc, gates, woa_t, wob_t, woc_t, wout, gffn, rw_t, rb_col, n_ctx_tiles):
    bsz, ntok, d = xs.shape
    nt = ntok // TM
    const = lambda b, i: (0, 0)
    tile4 = lambda b, i: (b, i, 0, 0)
    tok3 = lambda b, i: (b, i, 0)
    return pl.pallas_call(
        _merge_kernel,
        grid=(bsz, nt),
        in_specs=[pl.BlockSpec((1, TM, d), tok3),
                  pl.BlockSpec((1, 1, N_MOD, d), lambda b, i: (b, jnp.where(i < n_ctx_tiles, 0, 1), 0, 0)),
                  pl.BlockSpec((1, 1, ATTN_WIDTH, TM), tile4),
                  pl.BlockSpec((1, 1, ATTN_WIDTH, TM), tile4),
                  pl.BlockSpec((1, 1, ATTN_WIDTH, TM), tile4),
                  pl.BlockSpec((1, 1, 3 * d, TM), tile4),
                  pl.BlockSpec(woa_t.shape, const),
                  pl.BlockSpec(wob_t.shape, const),
                  pl.BlockSpec(woc_t.shape, const),
                  pl.BlockSpec(wout.shape, const),
                  pl.BlockSpec((1, d), const),
                  pl.BlockSpec(rw_t.shape, const),
                  pl.BlockSpec(rb_col.shape, const)],
        out_specs=[pl.BlockSpec((1, TM, d), tok3),
                   pl.BlockSpec((1, TM, d), tok3),
                   pl.BlockSpec((1, 1, TOP_K, TM), tile4),
                   pl.BlockSpec((1, 1, TOP_K, TM), tile4)],
        out_shape=[jax.ShapeDtypeStruct((bsz, ntok, d), F32),
                   jax.ShapeDtypeStruct((bsz, ntok, d), F32),
                   jax.ShapeDtypeStruct((bsz, nt, TOP_K, TM), jnp.int32),
                   jax.ShapeDtypeStruct((bsz, nt, TOP_K, TM), F32)],
        compiler_params=_cparams(("arbitrary", "arbitrary")),
        name="merge_router",
    )(xs, modtab, oa, ob, oc, gates, woa_t, wob_t, woc_t, wout, gffn, rw_t, rb_col)


def _moe_kernel(ib_ref, ie_ref, lo_ref, hi_ref, first_ref, nused_ref,
                dst_prev_ref, src_cur_ref, dst_cur_ref, src_next_ref,
                x_hbm, wgu_ref, bgu_ref, wdn_ref, bdn_ref, y_hbm,
                xbuf0, xbuf1, ybuf0, ybuf1, wgu_bf, wdn_bf, gsem, ssem, *, spare_row0):
    i = pl.program_id(0)
    n_used = nused_ref[0]
    d = D_MODEL

    def gather_row(src_ref, r, xb, sem):
        return pltpu.make_async_copy(x_hbm.at[pl.ds(src_ref[0, 0, r], 1), :], xb.at[pl.ds(r, 1), :], sem)

    def scatter_row(dst_ref, r, lo, hi, yb, sem):
        row = jnp.where((lo <= r) & (r < hi), dst_ref[0, 0, r], spare_row0 + r)
        return pltpu.make_async_copy(yb.at[pl.ds(r, 1), :], y_hbm.at[pl.ds(row, 1), :], sem)

    def wait_gather(xb, sem):
        pltpu.make_async_copy(x_hbm.at[pl.ds(0, MOE_TB), :], xb, sem).wait()

    def wait_scatter(yb, sem):
        pltpu.make_async_copy(yb, y_hbm.at[pl.ds(0, MOE_TB), :], sem).wait()

    def step(p):
        xb_cur, xb_nxt = (xbuf0, xbuf1) if p == 0 else (xbuf1, xbuf0)
        yb_cur, yb_prv = (ybuf0, ybuf1) if p == 0 else (ybuf1, ybuf0)
        q = 1 - p

        if p == 0:
            @pl.when(i == 0)
            def _():
                for r in range(MOE_TB):
                    gather_row(src_cur_ref, r, xb_cur, gsem.at[p]).start()
                yb_prv[...] = jnp.zeros_like(yb_prv)
                spare = pltpu.make_async_copy(yb_prv, y_hbm.at[pl.ds(spare_row0, MOE_TB), :], ssem.at[q])
                spare.start()
                spare.wait()

        @pl.when(first_ref[i] == 1)
        def _():
            wgu_bf[...] = wgu_ref[0, 0].astype(BF16)
            wdn_bf[...] = wdn_ref[0, 0].astype(BF16)

        wait_gather(xb_cur, gsem.at[p])

        @pl.when(i >= 1)
        def _():
            wait_scatter(yb_cur, ssem.at[p])

        ip = jnp.maximum(i - 1, 0)
        lo_prev = lo_ref[ip]
        hi_prev = jnp.where(i >= 1, hi_ref[ip], lo_prev)
        xb = xb_cur[...].astype(BF16)
        n_chunks = d // MOE_CW
        rows_per_chunk = MOE_TB // n_chunks
        y = None
        for c in range(n_chunks):
            for r in range(c * rows_per_chunk, (c + 1) * rows_per_chunk):
                gather_row(src_next_ref, r, xb_nxt, gsem.at[q]).start()
                scatter_row(dst_prev_ref, r, lo_prev, hi_prev, yb_prv, ssem.at[q]).start()
            c0, c1 = c * MOE_CW, (c + 1) * MOE_CW
            glu = jnp.dot(xb, wgu_bf[:, c0:c1], preferred_element_type=F32) + bgu_ref[0, 0, :, c0:c1]
            lin = jnp.dot(xb, wgu_bf[:, d + c0:d + c1], preferred_element_type=F32) + bgu_ref[0, 0, :, d + c0:d + c1]
            glu = jnp.minimum(glu, SWIGLU_LIMIT)
            lin = jnp.clip(lin, -SWIGLU_LIMIT, SWIGLU_LIMIT)
            act = glu * jax.nn.sigmoid(SWIGLU_ALPHA * glu) * (lin + 1.0)
            part = jnp.dot(act.astype(BF16), wdn_bf[c0:c1, :], preferred_element_type=F32)
            y = part if y is None else y + part
        yb_cur[...] = y + bdn_ref[0, 0]

        @pl.when(i == n_used - 1)
        def _():
            wait_scatter(yb_prv, ssem.at[q])
            wait_gather(xb_nxt, gsem.at[q])
            lo_cur, hi_cur = lo_ref[i], hi_ref[i]
            for r in range(MOE_TB):
                scatter_row(dst_cur_ref, r, lo_cur, hi_cur, yb_cur, ssem.at[p]).start()
            wait_scatter(yb_cur, ssem.at[p])

    for p in range(2):
        @pl.when((i < n_used) & (i % 2 == p))
        def _(p=p):
            step(p)


def _moe_experts(h2_flat, plan, w_gu, b_gu, w_dn, b_dn, layer):
    item_b, item_e, lo, hi, first, n_used, src_sorted, dst_sorted = plan
    n_items = item_b.shape[0]
    n_tok, d = h2_flat.shape
    nb = src_sorted.shape[0] // MOE_TB

    def idx_block(f):
        return pl.BlockSpec((1, 1, MOE_TB), lambda i, ib, ie, lo_, hi_, fs, nu: (ib[f(i)], 0, 0),
                            memory_space=pltpu.SMEM)
    wmap = lambda i, ib, ie, lo_, hi_, fs, nu: (layer, ie[i], 0, 0)
    grid_spec = pltpu.PrefetchScalarGridSpec(
        num_scalar_prefetch=6,
        grid=(n_items,),
        in_specs=[idx_block(lambda i: jnp.maximum(i - 1, 0)),
                  idx_block(lambda i: i),
                  idx_block(lambda i: i),
                  idx_block(lambda i: jnp.minimum(i + 1, n_items - 1)),
                  pl.BlockSpec(memory_space=pl.ANY),
                  pl.BlockSpec((1, 1, d, 2 * d), wmap),
                  pl.BlockSpec((1, 1, 1, 2 * d), wmap),
                  pl.BlockSpec((1, 1, d, d), wmap),
                  pl.BlockSpec((1, 1, 1, d), wmap)],
        out_specs=pl.BlockSpec(memory_space=pl.ANY),
        scratch_shapes=[pltpu.VMEM((MOE_TB, d), F32), pltpu.VMEM((MOE_TB, d), F32),
                        pltpu.VMEM((MOE_TB, d), F32), pltpu.VMEM((MOE_TB, d), F32),
                        pltpu.VMEM((d, 2 * d), BF16), pltpu.VMEM((d, d), BF16),
                        pltpu.SemaphoreType.DMA((2,)), pltpu.SemaphoreType.DMA((2,))],
    )
    depth, ne = b_gu.shape[0], b_gu.shape[1]
    src3 = src_sorted.reshape(nb, 1, MOE_TB)
    dst3 = dst_sorted.reshape(nb, 1, MOE_TB)
    return pl.pallas_call(
        functools.partial(_moe_kernel, spare_row0=TOP_K * n_tok),
        grid_spec=grid_spec,
        out_shape=jax.ShapeDtypeStruct((TOP_K * n_tok + MOE_TB, d), F32),
        compiler_params=pltpu.CompilerParams(dimension_semantics=("arbitrary",), vmem_limit_bytes=VMEM_LIMIT,
                                             has_side_effects=True),
        name="moe_experts",
    )(item_b, item_e, lo, hi, first, n_used, dst3, src3, dst3, src3, h2_flat,
      w_gu, b_gu.reshape(depth, ne, 1, 2 * d), w_dn, b_dn.reshape(depth, ne, 1, d))


def _moe_plan(top_idx):
    i32 = jnp.int32
    flat_e = top_idx.reshape(-1)
    n_assign = flat_e.shape[0]
    assert n_assign % MOE_TB == 0
    nb = n_assign // MOE_TB
    n_items = nb + N_EXPERTS
    order = jnp.argsort(flat_e, stable=True).astype(i32)
    counts = jnp.sum((flat_e[None, :] == jnp.arange(N_EXPERTS, dtype=i32)[:, None]).astype(i32), axis=1)
    ends = jnp.cumsum(counts)
    starts = ends - counts
    pos = jnp.sort(jnp.concatenate([jnp.arange(nb, dtype=i32) * MOE_TB, starts]))
    length = jnp.concatenate([pos[1:], jnp.full((1,), n_assign, i32)]) - pos
    keep = length > 0
    perm = jnp.argsort(jnp.where(keep, pos, n_assign + jnp.arange(n_items, dtype=i32)))
    pos, length, keep = pos[perm], length[perm], keep[perm]
    item_b = jnp.minimum(pos // MOE_TB, nb - 1)
    item_e = jnp.minimum(jnp.sum((pos[:, None] >= ends[None, :]).astype(i32), axis=1), N_EXPERTS - 1)
    lo = pos - item_b * MOE_TB
    hi = lo + length
    prev_e = jnp.concatenate([jnp.full((1,), -1, i32), item_e[:-1]])
    first = (keep & (item_e != prev_e)).astype(i32)
    n_used = jnp.sum(keep.astype(i32)).reshape(1)
    n_tok = n_assign // TOP_K
    k = lax.shift_right_logical(order, _TM_SHIFT) & (TOP_K - 1)
    tok = lax.shift_left(lax.shift_right_logical(order, _TM_SHIFT + _K_SHIFT), _TM_SHIFT) | (order & (TM - 1))
    return (item_b.astype(i32), item_e.astype(i32), lo.astype(i32), hi.astype(i32), first, n_used,
            tok.astype(i32), (k * n_tok + tok).astype(i32))


def _combine_kernel(x_ref, mod_ref, tw_ref, y0_ref, y1_ref, y2_ref, y3_ref, o_ref):
    tm = x_ref.shape[1]
    w8 = jnp.concatenate([tw_ref[0, 0], jnp.zeros((8 - TOP_K, tm), F32)], axis=0)
    wt = w8.T
    tot = (wt[:, 0:1] * y0_ref[...] + wt[:, 1:2] * y1_ref[...]) + (wt[:, 2:3] * y2_ref[...] + wt[:, 3:4] * y3_ref[...])
    o_ref[0] = x_ref[0] + mod_ref[0, 0, 5:6, :] * tot


def _combine(xs, modtab, top_w, y, n_ctx_tiles, skip_tiles):
    bsz, ntok, d = xs.shape
    nt = ntok // TM
    n_out = nt - skip_tiles
    yrow = lambda k: pl.BlockSpec((TM, d), lambda b, i: ((k * bsz + b) * nt + i + skip_tiles, 0))
    return pl.pallas_call(
        _combine_kernel,
        grid=(bsz, n_out),
        in_specs=[pl.BlockSpec((1, TM, d), lambda b, i: (b, i + skip_tiles, 0)),
                  pl.BlockSpec((1, 1, N_MOD, d),
                               lambda b, i: (b, jnp.where(i + skip_tiles < n_ctx_tiles, 0, 1), 0, 0)),
                  pl.BlockSpec((1, 1, TOP_K, TM), lambda b, i: (b, i + skip_tiles, 0, 0)),
                  yrow(0), yrow(1), yrow(2), yrow(3)],
        out_specs=pl.BlockSpec((1, TM, d), lambda b, i: (b, i, 0)),
        out_shape=jax.ShapeDtypeStruct((bsz, n_out * TM, d), F32),
        compiler_params=_cparams(("arbitrary", "arbitrary")),
        name="moe_combine",
    )(xs, modtab, top_w, y, y, y, y)


def _rope_table(n_ctx, n_lat):
    n_rows = n_lat // GRID_W
    row = jnp.repeat(jnp.arange(n_rows, dtype=F32), GRID_W)
    col = jnp.tile(jnp.arange(GRID_W, dtype=F32), n_rows)

    def table(rot_dim):
        axis_pairs = rot_dim // 4
        inv_freq = ROPE_THETA ** (-jnp.arange(axis_pairs, dtype=F32) / axis_pairs)
        ang = jnp.concatenate([row[:, None] * inv_freq, col[:, None] * inv_freq], axis=-1)
        cos = jnp.concatenate([jnp.ones((n_ctx, 2 * axis_pairs), F32), jnp.cos(ang)], axis=0)
        sin = jnp.concatenate([jnp.zeros((n_ctx, 2 * axis_pairs), F32), jnp.sin(ang)], axis=0)
        return cos, sin

    c64, s64 = table(A_DIM)
    c32, s32 = table(B_ROPE)
    tab = jnp.concatenate([c64, s64, c32, s32], axis=-1)
    ntok = n_ctx + n_lat
    return jnp.transpose(tab.reshape(ntok // TM, TM, tab.shape[1]), (0, 2, 1))


def _cols(v):
    return jnp.broadcast_to(v.astype(F32)[:, None], (v.shape[0], TM))


def kernel(x, c, ctx, c_ctx, w_mod, b_mod, norm_mix, norm_ffn, w_in, a_q_norm, a_k_norm, b_q_a_norm,
           b_kv_a_norm, b_w_uq, b_w_ukv, b_q_norm, b_k_norm, c_q_norm, c_k_norm, c_lambda, c_subln,
           w_o_a, w_o_b, w_o_c, w_out, router_w, router_b, exp_w_gu, exp_b_gu, exp_w_down, exp_b_down):
    bsz, n_lat, d = x.shape
    n_ctx = ctx.shape[1]
    depth = w_mod.shape[0]
    assert d == D_MODEL and n_ctx % TM == 0 and n_lat % TM == 0 and n_lat % GRID_W == 0 and bsz + 1 <= 8
    n_ctx_tiles = n_ctx // TM
    ntok = n_ctx + n_lat

    xs = jnp.concatenate([ctx, x], axis=1)
    cond8 = jnp.zeros((8, d), F32).at[:bsz].set(c).at[bsz].set(c_ctx)
    mod_all = _adaln(cond8, w_mod, b_mod).reshape(depth, 8, N_MOD, d)
    rope = _rope_table(n_ctx, n_lat)

    for l in range(depth):
        lam_init = 0.8 - 0.6 * math.exp(-0.3 * l)
        lq1, lk1, lq2, lk2 = c_lambda[l].astype(F32)
        lam = (jnp.exp(jnp.sum(lq1 * lk1)) - jnp.exp(jnp.sum(lq2 * lk2)) + lam_init).reshape(1)
        modtab = jnp.stack([jnp.broadcast_to(mod_all[l, bsz], (bsz, N_MOD, d)), mod_all[l, :bsz]], axis=1)
        gcol = jnp.concatenate([_cols(a_q_norm[l]), _cols(a_k_norm[l]), _cols(b_q_a_norm[l]),
                                _cols(b_kv_a_norm[l]), _cols(b_q_norm[l]), _cols(b_k_norm[l]),
                                _cols(c_q_norm[l]), _cols(c_k_norm[l])], axis=0)
        (qa, ka, va, qb, kb, vb, qc, kc, vc, gates) = _pre_mixer(
            xs, modtab, norm_mix[l].reshape(1, d), w_in[l].T.astype(BF16), b_w_uq[l].T.astype(BF16),
            b_w_ukv[l].T.astype(BF16), gcol, rope, n_ctx_tiles)

        gsub = _cols(c_subln[l])
        attn = functools.partial(_attention, lam, gsub=gsub, n_ctx_tiles=n_ctx_tiles)
        oa = attn(qa, ka, va, units=_UNITS_A, diff=False, out_scale=1.0, name="attn_gqa")
        ob = attn(qb, kb, vb, units=_UNITS_B, diff=False, out_scale=1.0, name="attn_mla")
        oc = attn(qc, kc, vc, units=_UNITS_C, diff=True, out_scale=1.0 - lam_init, name="attn_diff")

        xs, h2, top_idx, top_w = _merge(
            xs, modtab, oa, ob, oc, gates, w_o_a[l].T.astype(BF16), w_o_b[l].T.astype(BF16),
            w_o_c[l].T.astype(BF16), w_out[l].astype(BF16), norm_ffn[l].reshape(1, d),
            router_w[l].T.astype(BF16), _cols(router_b[l]), n_ctx_tiles)

        y = _moe_experts(h2.reshape(bsz * ntok, d), _moe_plan(top_idx), exp_w_gu, exp_b_gu, exp_w_down,
                         exp_b_down, l)
        last = l == depth - 1
        xs = _combine(xs, modtab, top_w, y, n_ctx_tiles, n_ctx_tiles if last else 0)
    return xs
```

```python
import functools
import math

import numpy as np
import jax
import jax.numpy as jnp
from jax import lax
from jax.experimental import pallas as pl
from jax.experimental.pallas import tpu as pltpu

F32 = jnp.float32
BF16 = jnp.bfloat16

D_MODEL = 1024
GRID_W = 64
ROPE_THETA = 10000.0
NORM_EPS = 1e-6
N_MOD = 6
A_HEADS, A_KV_HEADS, A_DIM = 8, 2, 64
B_HEADS, B_Q_RANK, B_KV_RANK, B_NOPE, B_ROPE, B_V = 8, 256, 128, 64, 32, 64
MLA_SCALE = (B_NOPE + B_ROPE) ** -0.5
LOG2E = math.log2(math.e)
C_HEADS, C_DIM, C_V = 8, 32, 64
N_EXPERTS, TOP_K = 32, 4
SWIGLU_ALPHA, SWIGLU_LIMIT = 1.702, 7.0
HEAD_V = 64
ATTN_WIDTH = 512

_IN_SIZES = (A_HEADS * A_DIM, A_KV_HEADS * A_DIM, A_KV_HEADS * A_DIM, B_Q_RANK, B_KV_RANK, B_ROPE,
             C_HEADS * 2 * C_DIM, C_HEADS * 2 * C_DIM, C_HEADS * C_V, 3 * D_MODEL)
_OFF = tuple(int(v) for v in np.cumsum((0,) + _IN_SIZES))
(_O_AQ, _O_AK, _O_AV, _O_BCQ, _O_BCKV, _O_BKR, _O_CQ, _O_CK, _O_CV, _O_GATE, _O_END) = _OFF
B_QW = B_NOPE + B_ROPE

_G_SIZES = (A_DIM, A_DIM, B_Q_RANK, B_KV_RANK, B_QW, B_QW, C_DIM, C_DIM)
_GOFF = tuple(int(v) for v in np.cumsum((0,) + _G_SIZES))
(_G_AQ, _G_AK, _G_BQA, _G_BKVA, _G_BQ, _G_BK, _G_CQ, _G_CK, _G_END) = _GOFF

LANES = 128
TM = 256
MOE_TB = 256
MOE_CW = 256
_TM_SHIFT = TM.bit_length() - 1
_K_SHIFT = TOP_K.bit_length() - 1
assert TM == 1 << _TM_SHIFT and TOP_K == 1 << _K_SHIFT
ATTN_GROUP = 16
ATTN_AHEAD = 2
ATTN_LAG = 1
ATTN_KS = 3
VMEM_LIMIT = 56 * 1024 * 1024

_NT = (((1,), (1,)), ((), ()))
_TN = (((0,), (0,)), ((), ()))


def _cparams(sem):
    return pltpu.CompilerParams(dimension_semantics=sem, vmem_limit_bytes=VMEM_LIMIT)


def _adaln_kernel(cond_ref, w_ref, b_ref, o_ref):
    cnd = cond_ref[...]
    s = (cnd * jax.nn.sigmoid(cnd)).astype(BF16)
    o_ref[0] = jnp.dot(s, w_ref[0].astype(BF16), preferred_element_type=F32) + b_ref[0]


def _adaln(cond8, w_mod, b_mod):
    depth, d, nm = w_mod.shape
    tn = 1536
    return pl.pallas_call(
        _adaln_kernel,
        grid=(depth, nm // tn),
        in_specs=[pl.BlockSpec((8, d), lambda l, j: (0, 0)),
                  pl.BlockSpec((1, d, tn), lambda l, j: (l, 0, j)),
                  pl.BlockSpec((1, 1, tn), lambda l, j: (l, 0, j))],
        out_specs=pl.BlockSpec((1, 8, tn), lambda l, j: (l, 0, j)),
        out_shape=jax.ShapeDtypeStruct((depth, 8, nm), F32),
        compiler_params=_cparams(("arbitrary", "arbitrary")),
        name="adaln",
    )(cond8, w_mod, b_mod.reshape(depth, 1, nm))


def _rms_tokens(x, gain):
    ms = jnp.mean(x * x, axis=-1, keepdims=True)
    return x * lax.rsqrt(ms + NORM_EPS) * gain


def _rms_rows(x, g):
    ms = jnp.mean(x * x, axis=0, keepdims=True)
    return x * lax.rsqrt(ms + NORM_EPS) * g


def _rope_rows(y, cos, sin):
    r = y.shape[0] // 2
    y1, y2 = y[:r], y[r:]
    return y1 * cos - y2 * sin, y2 * cos + y1 * sin


def _pre_kernel(x_ref, mod_ref, gmix_ref, w_ref, wuq_ref, wukv_ref, gcol_ref, rope_ref,
                qa_ref, ka_ref, va_ref, qb_ref, kb_ref, vb_ref, qc_ref, kc_ref, vc_ref, gates_ref):
    x = x_ref[0]
    sh1 = mod_ref[0, 0, 0:1, :]
    sc1 = mod_ref[0, 0, 1:2, :]
    h = _rms_tokens(x, gmix_ref[...]) * (1.0 + sc1) + sh1
    hb = h.astype(BF16)
    proj = lax.dot_general(w_ref[0:_O_GATE, :], hb, _NT, preferred_element_type=F32)
    gates = lax.dot_general(w_ref[_O_GATE:_O_END, :], hb, _NT, preferred_element_type=F32)
    gates_ref[0, 0] = jax.nn.sigmoid(gates).astype(BF16)

    cos64, sin64 = rope_ref[0, 0:32, :], rope_ref[0, 32:64, :]
    cos32, sin32 = rope_ref[0, 64:80, :], rope_ref[0, 80:96, :]

    def gcol(a, b):
        return gcol_ref[a:b, :]

    g_aq, g_ak = gcol(_G_AQ, _G_AK), gcol(_G_AK, _G_BQA)
    a_scale = A_DIM ** -0.5 * LOG2E
    for hd in range(A_HEADS):
        r0 = _O_AQ + hd * A_DIM
        o1, o2 = _rope_rows(_rms_rows(proj[r0:r0 + A_DIM], g_aq), cos64, sin64)
        qa_ref[0, 0, hd * A_DIM:hd * A_DIM + 32, :] = (o1 * a_scale).astype(BF16)
        qa_ref[0, 0, hd * A_DIM + 32:(hd + 1) * A_DIM, :] = (o2 * a_scale).astype(BF16)
    k_parts = []
    for g in range(A_KV_HEADS):
        r0 = _O_AK + g * A_DIM
        o1, o2 = _rope_rows(_rms_rows(proj[r0:r0 + A_DIM], g_ak), cos64, sin64)
        k_parts += [o1, o2]
    ka_ref[0] = jnp.concatenate(k_parts, axis=0).T.astype(BF16)
    va_ref[0, 0] = proj[_O_AV:_O_BCQ].astype(BF16)

    g_bq, g_bk = gcol(_G_BQ, _G_BK), gcol(_G_BK, _G_CQ)
    b_scale = MLA_SCALE * LOG2E
    cqn = _rms_rows(proj[_O_BCQ:_O_BCKV], gcol(_G_BQA, _G_BKVA)).astype(BF16)
    bq = jnp.dot(wuq_ref[...], cqn, preferred_element_type=F32)
    for hd in range(B_HEADS):
        r0 = hd * B_QW
        qn = _rms_rows(bq[r0:r0 + B_NOPE], g_bq[:B_NOPE])
        qr = _rms_rows(bq[r0 + B_NOPE:r0 + B_QW], g_bq[B_NOPE:])
        r1, r2 = _rope_rows(qr, cos32, sin32)
        qb_ref[0, 0, r0:r0 + B_NOPE, :] = (qn * b_scale).astype(BF16)
        qb_ref[0, 0, r0 + B_NOPE:r0 + B_NOPE + 16, :] = (r1 * b_scale).astype(BF16)
        qb_ref[0, 0, r0 + B_NOPE + 16:r0 + B_QW, :] = (r2 * b_scale).astype(BF16)
    ckvn = _rms_rows(proj[_O_BCKV:_O_BKR], gcol(_G_BKVA, _G_BQ)).astype(BF16)
    bkv = jnp.dot(wukv_ref[...], ckvn, preferred_element_type=F32)
    kr = _rms_rows(proj[_O_BKR:_O_CQ], g_bk[B_NOPE:])
    kr1, kr2 = _rope_rows(kr, cos32, sin32)
    zpad = jnp.zeros((LANES - B_QW, kr1.shape[1]), F32)
    k_parts = []
    for hd in range(B_HEADS):
        r0 = hd * (B_NOPE + B_V)
        kn = _rms_rows(bkv[r0:r0 + B_NOPE], g_bk[:B_NOPE])
        k_parts += [kn, kr1, kr2, zpad]
        vb_ref[0, 0, hd * B_V:(hd + 1) * B_V, :] = bkv[r0 + B_NOPE:r0 + B_NOPE + B_V].astype(BF16)
    kb_ref[0] = jnp.concatenate(k_parts, axis=0).T.astype(BF16)

    g_cq, g_ck = gcol(_G_CQ, _G_CK), gcol(_G_CK, _G_END)
    c_scale = C_DIM ** -0.5 * LOG2E
    k_parts = []
    for j in range(2 * C_HEADS):
        r0 = _O_CQ + j * C_DIM
        o1, o2 = _rope_rows(_rms_rows(proj[r0:r0 + C_DIM], g_cq), cos32, sin32)
        qc_ref[0, 0, j * C_DIM:j * C_DIM + 16, :] = (o1 * c_scale).astype(BF16)
        qc_ref[0, 0, j * C_DIM + 16:(j + 1) * C_DIM, :] = (o2 * c_scale).astype(BF16)
        r0 = _O_CK + j * C_DIM
        o1, o2 = _rope_rows(_rms_rows(proj[r0:r0 + C_DIM], g_ck), cos32, sin32)
        k_parts += [o1, o2]
    kc_ref[0] = jnp.concatenate(k_parts, axis=0).T.astype(BF16)
    vc_ref[0, 0] = proj[_O_CV:_O_GATE].astype(BF16)


def _pre_mixer(xs, modtab, gmix, w_int, wuq_t, wukv_t, gcol, rope, n_ctx_tiles):
    bsz, ntok, d = xs.shape
    nt = ntok // TM

    def fm(rows, dtype=BF16):
        return (jax.ShapeDtypeStruct((bsz, nt, rows, TM), dtype),
                pl.BlockSpec((1, 1, rows, TM), lambda b, i: (b, i, 0, 0)))

    def tmaj(cols):
        return (jax.ShapeDtypeStruct((bsz, ntok, cols), BF16),
                pl.BlockSpec((1, TM, cols), lambda b, i: (b, i, 0)))

    outs = [fm(A_HEADS * A_DIM), tmaj(LANES), fm(A_KV_HEADS * A_DIM),
            fm(B_HEADS * B_QW), tmaj(B_HEADS * LANES), fm(B_HEADS * B_V),
            fm(2 * C_HEADS * C_DIM), tmaj(2 * C_HEADS * C_DIM), fm(C_HEADS * C_V),
            fm(3 * D_MODEL)]
    const = lambda b, i: (0, 0)
    return pl.pallas_call(
        _pre_kernel,
        grid=(bsz, nt),
        in_specs=[pl.BlockSpec((1, TM, d), lambda b, i: (b, i, 0)),
                  pl.BlockSpec((1, 1, N_MOD, d), lambda b, i: (b, jnp.where(i < n_ctx_tiles, 0, 1), 0, 0)),
                  pl.BlockSpec((1, d), const),
                  pl.BlockSpec(w_int.shape, const),
                  pl.BlockSpec(wuq_t.shape, const),
                  pl.BlockSpec(wukv_t.shape, const),
                  pl.BlockSpec(gcol.shape, const),
                  pl.BlockSpec((1, rope.shape[1], TM), lambda b, i: (i, 0, 0))],
        out_specs=[o[1] for o in outs],
        out_shape=[o[0] for o in outs],
        compiler_params=_cparams(("arbitrary", "arbitrary")),
        name="pre_mixer",
    )(xs, modtab, gmix, w_int, wuq_t, wukv_t, gcol, rope)


def _attn_kernel(*refs, mixers, group_size, n_ctx_tiles, n_tiles):
    nm = len(mixers)
    lam_ref, gsub_ref = refs[0], refs[1 + 3 * nm]
    qkv = [refs[1 + 3 * m:4 + 3 * m] for m in range(nm)]
    o_refs = refs[2 + 3 * nm:2 + 4 * nm]
    s_scr = refs[2 + 4 * nm]
    i = pl.program_id(1)
    tq = qkv[0][0].shape[3]
    lat_ks = ATTN_KS if n_tiles % ATTN_KS == 0 else 1
    chains = [(m,) + u for m, (units, _, _) in enumerate(mixers) for u in units]

    def padded_q(m, q_row0, q_rows, pad_off):
        q = qkv[m][0][0, 0, q_row0:q_row0 + q_rows, :]
        parts = []
        if pad_off:
            parts.append(jnp.zeros((pad_off, tq), BF16))
        parts.append(q)
        rest = LANES - pad_off - q_rows
        if rest:
            parts.append(jnp.zeros((rest, tq), BF16))
        return jnp.concatenate(parts, axis=0) if len(parts) > 1 else q

    def run_group(group, ks, n_steps):
        qpads = [padded_q(c[0], c[1], c[2], c[3]) for c in group]
        n = len(group)
        pre = min(ATTN_AHEAD, n)
        rows = ks * TM

        def qk_one(j, at_step):
            row0 = pl.multiple_of(at_step * rows, rows)
            m, k_blk = group[j][0], group[j][4]
            k = qkv[m][1][0, pl.ds(row0, rows), k_blk * LANES:(k_blk + 1) * LANES]
            return jnp.dot(k, qpads[j], preferred_element_type=F32)

        def body(step, state):
            def softmax(j, s):
                m, l, _ = state[j]
                m_new = jnp.maximum(m, jnp.max(s, axis=0, keepdims=True))
                alpha = jnp.exp2(m - m_new)
                p = jnp.exp2(s - m_new)
                l = alpha * l + jnp.sum(p, axis=0, keepdims=True)
                return m_new, l, alpha, p.astype(BF16)

            def pv_one(j, sm):
                m_new, l, alpha, pb = sm
                mx, v_row0 = group[j][0], group[j][5]
                pv = None
                for t in range(ks):
                    v = qkv[mx][2][0, step * ks + t, v_row0:v_row0 + HEAD_V, :]
                    d = jnp.dot(v, pb[t * TM:(t + 1) * TM], preferred_element_type=F32)
                    pv = d if pv is None else pv + d
                return m_new, l, alpha * state[j][2] + pv

            nxt = jnp.minimum(step + 1, n_steps - 1)
            scores, sms, new = {}, {}, [None] * n
            for j in range(n):
                s = s_scr[j, 0:rows, :] if j < pre else scores.pop(j)
                sms[j] = softmax(j, s)
                if j + pre < n:
                    scores[j + pre] = qk_one(j + pre, step)
                else:
                    s_scr[j + pre - n, 0:rows, :] = qk_one(j + pre - n, nxt)
                if j - ATTN_LAG >= 0:
                    new[j - ATTN_LAG] = pv_one(j - ATTN_LAG, sms.pop(j - ATTN_LAG))
            for j in sorted(sms):
                new[j] = pv_one(j, sms[j])
            return tuple(new)

        for j in range(pre):
            s_scr[j, 0:rows, :] = qk_one(j, 0)
        init = tuple((jnp.full((1, tq), -1e30, F32), jnp.zeros((1, tq), F32), jnp.zeros((HEAD_V, tq), F32))
                     for _ in group)
        return [acc / l for (_, l, acc) in lax.fori_loop(0, n_steps, body, init)]

    def run_tile(ks, n_steps):
        outs = []
        for g0 in range(0, len(chains), group_size):
            outs += run_group(chains[g0:g0 + group_size], ks, n_steps)
        c0 = 0
        for m, (units, diff, out_scale) in enumerate(mixers):
            per_head = 2 if diff else 1
            for hd in range(len(units) // per_head):
                if diff:
                    o = outs[c0 + 2 * hd] - lam_ref[0] * outs[c0 + 2 * hd + 1]
                    o = _rms_rows(o, gsub_ref[...]) * out_scale
                else:
                    o = outs[c0 + hd]
                o_refs[m][0, 0, hd * HEAD_V:(hd + 1) * HEAD_V, :] = o.astype(BF16)
            c0 += len(units)

    @pl.when(i < n_ctx_tiles)
    def _():
        run_tile(1, n_ctx_tiles)

    @pl.when(i >= n_ctx_tiles)
    def _():
        run_tile(lat_ks, n_tiles // lat_ks)


def _attention(lam, gsub, qkvs, mixers, *, n_ctx_tiles, name):
    bsz, nt = qkvs[0][0].shape[:2]
    kern = functools.partial(_attn_kernel, mixers=mixers, group_size=ATTN_GROUP, n_ctx_tiles=n_ctx_tiles,
                             n_tiles=nt)
    in_specs = [pl.BlockSpec(memory_space=pltpu.SMEM)]
    args = [lam]
    for qt, k, vt in qkvs:
        in_specs += [pl.BlockSpec((1, 1, qt.shape[2], TM), lambda b, i: (b, i, 0, 0)),
                     pl.BlockSpec((1,) + k.shape[1:], lambda b, i: (b, 0, 0), pipeline_mode=pl.Buffered(1)),
                     pl.BlockSpec((1,) + vt.shape[1:], lambda b, i: (b, 0, 0, 0), pipeline_mode=pl.Buffered(1))]
        args += [qt, k, vt]
    in_specs.append(pl.BlockSpec(gsub.shape, lambda b, i: (0, 0)))
    args.append(gsub)
    out_spec = pl.BlockSpec((1, 1, ATTN_WIDTH, TM), lambda b, i: (b, i, 0, 0))
    out_shape = jax.ShapeDtypeStruct((bsz, nt, ATTN_WIDTH, TM), BF16)
    return pl.pallas_call(
        kern,
        grid=(bsz, nt),
        in_specs=in_specs,
        out_specs=[out_spec] * len(qkvs),
        out_shape=[out_shape] * len(qkvs),
        scratch_shapes=[pltpu.VMEM((ATTN_AHEAD, ATTN_KS * TM, TM), F32)],
        compiler_params=_cparams(("arbitrary", "arbitrary")),
        name=name,
    )(*args)


_UNITS_A = tuple((h * A_DIM, A_DIM, A_DIM * (h // (A_HEADS // A_KV_HEADS)), 0,
                  HEAD_V * (h // (A_HEADS // A_KV_HEADS))) for h in range(A_HEADS))
_UNITS_B = tuple((h * B_QW, B_QW, 0, h, h * B_V) for h in range(B_HEADS))
_UNITS_C = tuple((h * 2 * C_DIM + c * C_DIM, C_DIM, C_DIM * (2 * (h % 2) + c), h // 2, h * C_V)
                 for h in range(C_HEADS) for c in range(2))


def _merge_kernel(x_ref, mod_ref, oa_ref, ob_ref, oc_ref, gates_ref, woa_ref, wob_ref, woc_ref, wout_ref,
                  gffn_ref, rw_ref, rb_ref, xo_ref, h2_ref, tidx_ref, tw_ref):
    d = D_MODEL
    ya = jnp.dot(woa_ref[...], oa_ref[0, 0], preferred_element_type=F32)
    yb = jnp.dot(wob_ref[...], ob_ref[0, 0], preferred_element_type=F32)
    yc = jnp.dot(woc_ref[...], oc_ref[0, 0], preferred_element_type=F32)
    yt = (gates_ref[0, 0, 0:d, :] * ya + gates_ref[0, 0, d:2 * d, :] * yb
          + gates_ref[0, 0, 2 * d:3 * d, :] * yc)
    out = lax.dot_general(yt.astype(BF16), wout_ref[...], _TN, preferred_element_type=F32)
    g1 = mod_ref[0, 0, 2:3, :]
    x = x_ref[0] + g1 * out
    xo_ref[0] = x
    sh2, sc2 = mod_ref[0, 0, 3:4, :], mod_ref[0, 0, 4:5, :]
    h2 = _rms_tokens(x, gffn_ref[...]) * (1.0 + sc2) + sh2
    h2_ref[0] = h2
    logits = lax.dot_general(rw_ref[...], h2.astype(BF16), _NT, preferred_element_type=F32) + rb_ref[...]
    iota = lax.broadcasted_iota(jnp.int32, logits.shape, 0)
    vals, idxs = [], []
    cur = logits
    for _ in range(TOP_K):
        mx = jnp.max(cur, axis=0, keepdims=True)
        ix = jnp.min(jnp.where(cur == mx, iota, N_EXPERTS), axis=0, keepdims=True)
        vals.append(mx)
        idxs.append(ix)
        cur = jnp.where(iota == ix, -jnp.inf, cur)
    tv = jnp.concatenate(vals, axis=0)
    e = jnp.exp(tv - tv[0:1])
    tw_ref[0, 0] = e / jnp.sum(e, axis=0, keepdims=True)
    tidx_ref[0, 0] = jnp.concatenate(idxs, axis=0)


def _merge(xs, modtab, oa, ob, oc, gates, woa_t, wob_t, woc_t, wout, gffn, rw_t, rb_col, n_ctx_tiles):
    bsz, ntok, d = xs.shape
    nt = ntok // TM
    const = lambda b, i: (0, 0)
    tile4 = lambda b, i: (b, i, 0, 0)
    tok3 = lambda b, i: (b, i, 0)
    return pl.pallas_call(
        _merge_kernel,
        grid=(bsz, nt),
        in_specs=[pl.BlockSpec((1, TM, d), tok3),
                  pl.BlockSpec((1, 1, N_MOD, d), lambda b, i: (b, jnp.where(i < n_ctx_tiles, 0, 1), 0, 0)),
                  pl.BlockSpec((1, 1, ATTN_WIDTH, TM), tile4),
                  pl.BlockSpec((1, 1, ATTN_WIDTH, TM), tile4),
                  pl.BlockSpec((1, 1, ATTN_WIDTH, TM), tile4),
                  pl.BlockSpec((1, 1, 3 * d, TM), tile4),
                  pl.BlockSpec(woa_t.shape, const),
                  pl.BlockSpec(wob_t.shape, const),
                  pl.BlockSpec(woc_t.shape, const),
                  pl.BlockSpec(wout.shape, const),
                  pl.BlockSpec((1, d), const),
                  pl.BlockSpec(rw_t.shape, const),
                  pl.BlockSpec(rb_col.shape, const)],
        out_specs=[pl.BlockSpec((1, TM, d), tok3),
                   pl.BlockSpec((1, TM, d), tok3),
                   pl.BlockSpec((1, 1, TOP_K, TM), tile4),
                   pl.BlockSpec((1, 1, TOP_K, TM), tile4)],
        out_shape=[jax.ShapeDtypeStruct((bsz, ntok, d), F32),
                   jax.ShapeDtypeStruct((bsz, ntok, d), F32),
                   jax.ShapeDtypeStruct((bsz, nt, TOP_K, TM), jnp.int32),
                   jax.ShapeDtypeStruct((bsz, nt, TOP_K, TM), F32)],
        compiler_params=_cparams(("arbitrary", "arbitrary")),
        name="merge_router",
    )(xs, modtab, oa, ob, oc, gates, woa_t, wob_t, woc_t, wout, gffn, rw_t, rb_col)


def _moe_kernel(ib_ref, ie_ref, lo_ref, hi_ref, first_ref, nused_ref,
                src_cur_ref, dst_cur_ref, src_next_ref,
                x_hbm, wgu_ref, bgu_ref, wdn_ref, bdn_ref, y_hbm,
                xbuf0, xbuf1, ybuf0, ybuf1, wgu_bf, wdn_bf, gsem, ssem, *, spare_row0):
    i = pl.program_id(0)
    n_used = nused_ref[0]
    d = D_MODEL

    def gather_row(src_ref, r, xb, sem):
        return pltpu.make_async_copy(x_hbm.at[pl.ds(src_ref[0, 0, r], 1), :], xb.at[pl.ds(r, 1), :], sem)

    def scatter_row(r, lo, hi, yb, sem, spare):
        row = jnp.where((lo <= r) & (r < hi), dst_cur_ref[0, 0, r], spare + r)
        return pltpu.make_async_copy(yb.at[pl.ds(r, 1), :], y_hbm.at[pl.ds(row, 1), :], sem)

    def wait_gather(xb, sem):
        pltpu.make_async_copy(x_hbm.at[pl.ds(0, MOE_TB), :], xb, sem).wait()

    def wait_scatter(yb, sem):
        pltpu.make_async_copy(yb, y_hbm.at[pl.ds(0, MOE_TB), :], sem).wait()

    def step(p):
        xb_cur, xb_nxt = (xbuf0, xbuf1) if p == 0 else (xbuf1, xbuf0)
        yb_cur, yb_prv = (ybuf0, ybuf1) if p == 0 else (ybuf1, ybuf0)
        q = 1 - p

        if p == 0:
            @pl.when(i == 0)
            def _():
                for r in range(MOE_TB):
                    gather_row(src_cur_ref, r, xb_cur, gsem.at[p]).start(priority=r % 2)
                yb_prv[...] = jnp.zeros_like(yb_prv)
                for half in range(2):
                    spare = pltpu.make_async_copy(
                        yb_prv, y_hbm.at[pl.ds(spare_row0 + half * MOE_TB, MOE_TB), :], ssem.at[q])
                    spare.start()
                    spare.wait()

        @pl.when(first_ref[i] == 1)
        def _():
            wgu_bf[...] = wgu_ref[0, 0].astype(BF16)
            wdn_bf[...] = wdn_ref[0, 0].astype(BF16)

        for r in range(MOE_TB):
            gather_row(src_next_ref, r, xb_nxt, gsem.at[q]).start(priority=r % 2)
        wait_gather(xb_cur, gsem.at[p])

        xb = xb_cur[...].astype(BF16)
        y = None
        for c in range(d // MOE_CW):
            c0, c1 = c * MOE_CW, (c + 1) * MOE_CW
            glu = jnp.dot(xb, wgu_bf[:, c0:c1], preferred_element_type=F32) + bgu_ref[0, 0, :, c0:c1]
            lin = jnp.dot(xb, wgu_bf[:, d + c0:d + c1], preferred_element_type=F32) + bgu_ref[0, 0, :, d + c0:d + c1]
            glu = jnp.minimum(glu, SWIGLU_LIMIT)
            lin = jnp.clip(lin, -SWIGLU_LIMIT, SWIGLU_LIMIT)
            act = glu * jax.nn.sigmoid(SWIGLU_ALPHA * glu) * (lin + 1.0)
            part = jnp.dot(act.astype(BF16), wdn_bf[c0:c1, :], preferred_element_type=F32)
            y = part if y is None else y + part
        y = y + bdn_ref[0, 0]

        @pl.when(i >= 2)
        def _():
            wait_scatter(yb_cur, ssem.at[p])

        yb_cur[...] = y
        lo_cur, hi_cur = lo_ref[i], hi_ref[i]
        for r in range(MOE_TB):
            scatter_row(r, lo_cur, hi_cur, yb_cur, ssem.at[p], spare_row0 + p * MOE_TB).start(priority=r % 2)

        @pl.when(i == n_used - 1)
        def _():
            @pl.when(i >= 1)
            def _():
                wait_scatter(yb_prv, ssem.at[q])
            wait_scatter(yb_cur, ssem.at[p])
            wait_gather(xb_nxt, gsem.at[q])

    for p in range(2):
        @pl.when((i < n_used) & (i % 2 == p))
        def _(p=p):
            step(p)


def _moe_experts(h2_flat, plan, w_gu, b_gu, w_dn, b_dn, layer):
    item_b, item_e, lo, hi, first, n_used, src_sorted, dst_sorted = plan
    n_items = item_b.shape[0]
    n_tok, d = h2_flat.shape
    nb = src_sorted.shape[0] // MOE_TB

    def idx_block(f):
        return pl.BlockSpec((1, 1, MOE_TB), lambda i, ib, ie, lo_, hi_, fs, nu: (ib[f(i)], 0, 0),
                            memory_space=pltpu.SMEM)
    wmap = lambda i, ib, ie, lo_, hi_, fs, nu: (layer, ie[i], 0, 0)
    grid_spec = pltpu.PrefetchScalarGridSpec(
        num_scalar_prefetch=6,
        grid=(n_items,),
        in_specs=[idx_block(lambda i: i),
                  idx_block(lambda i: i),
                  idx_block(lambda i: jnp.minimum(i + 1, n_items - 1)),
                  pl.BlockSpec(memory_space=pl.ANY),
                  pl.BlockSpec((1, 1, d, 2 * d), wmap),
                  pl.BlockSpec((1, 1, 1, 2 * d), wmap),
                  pl.BlockSpec((1, 1, d, d), wmap),
                  pl.BlockSpec((1, 1, 1, d), wmap)],
        out_specs=pl.BlockSpec(memory_space=pl.ANY),
        scratch_shapes=[pltpu.VMEM((MOE_TB, d), F32), pltpu.VMEM((MOE_TB, d), F32),
                        pltpu.VMEM((MOE_TB, d), F32), pltpu.VMEM((MOE_TB, d), F32),
                        pltpu.VMEM((d, 2 * d), BF16), pltpu.VMEM((d, d), BF16),
                        pltpu.SemaphoreType.DMA((2,)), pltpu.SemaphoreType.DMA((2,))],
    )
    depth, ne = b_gu.shape[0], b_gu.shape[1]
    src3 = src_sorted.reshape(nb, 1, MOE_TB)
    dst3 = dst_sorted.reshape(nb, 1, MOE_TB)
    return pl.pallas_call(
        functools.partial(_moe_kernel, spare_row0=TOP_K * n_tok),
        grid_spec=grid_spec,
        out_shape=jax.ShapeDtypeStruct((TOP_K * n_tok + 2 * MOE_TB, d), F32),
        compiler_params=pltpu.CompilerParams(dimension_semantics=("arbitrary",), vmem_limit_bytes=VMEM_LIMIT,
                                             has_side_effects=True),
        name="moe_experts",
    )(item_b, item_e, lo, hi, first, n_used, src3, dst3, src3, h2_flat,
      w_gu, b_gu.reshape(depth, ne, 1, 2 * d), w_dn, b_dn.reshape(depth, ne, 1, d))


def _moe_plan(top_idx):
    i32 = jnp.int32
    flat_e = top_idx.reshape(-1)
    n_assign = flat_e.shape[0]
    assert n_assign % MOE_TB == 0
    nb = n_assign // MOE_TB
    n_items = nb + N_EXPERTS
    order = jnp.argsort(flat_e, stable=True).astype(i32)
    counts = jnp.sum((flat_e[None, :] == jnp.arange(N_EXPERTS, dtype=i32)[:, None]).astype(i32), axis=1)
    ends = jnp.cumsum(counts)
    starts = ends - counts
    pos = jnp.sort(jnp.concatenate([jnp.arange(nb, dtype=i32) * MOE_TB, starts]))
    length = jnp.concatenate([pos[1:], jnp.full((1,), n_assign, i32)]) - pos
    keep = length > 0
    perm = jnp.argsort(jnp.where(keep, pos, n_assign + jnp.arange(n_items, dtype=i32)))
    pos, length, keep = pos[perm], length[perm], keep[perm]
    item_b = jnp.minimum(pos // MOE_TB, nb - 1)
    item_e = jnp.minimum(jnp.sum((pos[:, None] >= ends[None, :]).astype(i32), axis=1), N_EXPERTS - 1)
    lo = pos - item_b * MOE_TB
    hi = lo + length
    prev_e = jnp.concatenate([jnp.full((1,), -1, i32), item_e[:-1]])
    first = (keep & (item_e != prev_e)).astype(i32)
    n_used = jnp.sum(keep.astype(i32)).reshape(1)
    n_tok = n_assign // TOP_K
    k = lax.shift_right_logical(order, _TM_SHIFT) & (TOP_K - 1)
    tok = lax.shift_left(lax.shift_right_logical(order, _TM_SHIFT + _K_SHIFT), _TM_SHIFT) | (order & (TM - 1))
    return (item_b.astype(i32), item_e.astype(i32), lo.astype(i32), hi.astype(i32), first, n_used,
            tok.astype(i32), (k * n_tok + tok).astype(i32))


def _combine_kernel(x_ref, mod_ref, tw_ref, y0_ref, y1_ref, y2_ref, y3_ref, o_ref):
    tm = x_ref.shape[1]
    w8 = jnp.concatenate([tw_ref[0, 0], jnp.zeros((8 - TOP_K, tm), F32)], axis=0)
    wt = w8.T
    tot = (wt[:, 0:1] * y0_ref[...] + wt[:, 1:2] * y1_ref[...]) + (wt[:, 2:3] * y2_ref[...] + wt[:, 3:4] * y3_ref[...])
    o_ref[0] = x_ref[0] + mod_ref[0, 0, 5:6, :] * tot


def _combine(xs, modtab, top_w, y, n_ctx_tiles, skip_tiles):
    bsz, ntok, d = xs.shape
    nt = ntok // TM
    n_out = nt - skip_tiles
    yrow = lambda k: pl.BlockSpec((TM, d), lambda b, i: ((k * bsz + b) * nt + i + skip_tiles, 0))
    return pl.pallas_call(
        _combine_kernel,
        grid=(bsz, n_out),
        in_specs=[pl.BlockSpec((1, TM, d), lambda b, i: (b, i + skip_tiles, 0)),
                  pl.BlockSpec((1, 1, N_MOD, d),
                               lambda b, i: (b, jnp.where(i + skip_tiles < n_ctx_tiles, 0, 1), 0, 0)),
                  pl.BlockSpec((1, 1, TOP_K, TM), lambda b, i: (b, i + skip_tiles, 0, 0)),
                  yrow(0), yrow(1), yrow(2), yrow(3)],
        out_specs=pl.BlockSpec((1, TM, d), lambda b, i: (b, i, 0)),
        out_shape=jax.ShapeDtypeStruct((bsz, n_out * TM, d), F32),
        compiler_params=_cparams(("arbitrary", "arbitrary")),
        name="moe_combine",
    )(xs, modtab, top_w, y, y, y, y)


def _rope_table(n_ctx, n_lat):
    n_rows = n_lat // GRID_W
    row = jnp.repeat(jnp.arange(n_rows, dtype=F32), GRID_W)
    col = jnp.tile(jnp.arange(GRID_W, dtype=F32), n_rows)

    def table(rot_dim):
        axis_pairs = rot_dim // 4
        inv_freq = ROPE_THETA ** (-jnp.arange(axis_pairs, dtype=F32) / axis_pairs)
        ang = jnp.concatenate([row[:, None] * inv_freq, col[:, None] * inv_freq], axis=-1)
        cos = jnp.concatenate([jnp.ones((n_ctx, 2 * axis_pairs), F32), jnp.cos(ang)], axis=0)
        sin = jnp.concatenate([jnp.zeros((n_ctx, 2 * axis_pairs), F32), jnp.sin(ang)], axis=0)
        return cos, sin

    c64, s64 = table(A_DIM)
    c32, s32 = table(B_ROPE)
    tab = jnp.concatenate([c64, s64, c32, s32], axis=-1)
    ntok = n_ctx + n_lat
    return jnp.transpose(tab.reshape(ntok // TM, TM, tab.shape[1]), (0, 2, 1))


def _cols(v):
    return jnp.broadcast_to(v.astype(F32)[:, None], (v.shape[0], TM))


def kernel(x, c, ctx, c_ctx, w_mod, b_mod, norm_mix, norm_ffn, w_in, a_q_norm, a_k_norm, b_q_a_norm,
           b_kv_a_norm, b_w_uq, b_w_ukv, b_q_norm, b_k_norm, c_q_norm, c_k_norm, c_lambda, c_subln,
           w_o_a, w_o_b, w_o_c, w_out, router_w, router_b, exp_w_gu, exp_b_gu, exp_w_down, exp_b_down):
    bsz, n_lat, d = x.shape
    n_ctx = ctx.shape[1]
    depth = w_mod.shape[0]
    assert d == D_MODEL and n_ctx % TM == 0 and n_lat % TM == 0 and n_lat % GRID_W == 0 and bsz + 1 <= 8
    n_ctx_tiles = n_ctx // TM
    ntok = n_ctx + n_lat

    xs = jnp.concatenate([ctx, x], axis=1)
    cond8 = jnp.zeros((8, d), F32).at[:bsz].set(c).at[bsz].set(c_ctx)
    mod_all = _adaln(cond8, w_mod, b_mod).reshape(depth, 8, N_MOD, d)
    rope = _rope_table(n_ctx, n_lat)

    for l in range(depth):
        lam_init = 0.8 - 0.6 * math.exp(-0.3 * l)
        lq1, lk1, lq2, lk2 = c_lambda[l].astype(F32)
        lam = (jnp.exp(jnp.sum(lq1 * lk1)) - jnp.exp(jnp.sum(lq2 * lk2)) + lam_init).reshape(1)
        modtab = jnp.stack([jnp.broadcast_to(mod_all[l, bsz], (bsz, N_MOD, d)), mod_all[l, :bsz]], axis=1)
        gcol = jnp.concatenate([_cols(a_q_norm[l]), _cols(a_k_norm[l]), _cols(b_q_a_norm[l]),
                                _cols(b_kv_a_norm[l]), _cols(b_q_norm[l]), _cols(b_k_norm[l]),
                                _cols(c_q_norm[l]), _cols(c_k_norm[l])], axis=0)
        (qa, ka, va, qb, kb, vb, qc, kc, vc, gates) = _pre_mixer(
            xs, modtab, norm_mix[l].reshape(1, d), w_in[l].T.astype(BF16), b_w_uq[l].T.astype(BF16),
            b_w_ukv[l].T.astype(BF16), gcol, rope, n_ctx_tiles)

        gsub = _cols(c_subln[l])
        oa, ob = _attention(lam, gsub, [(qa, ka, va), (qb, kb, vb)],
                            ((_UNITS_A, False, 1.0), (_UNITS_B, False, 1.0)),
                            n_ctx_tiles=n_ctx_tiles, name="attn_gqa_mla")
        oc, = _attention(lam, gsub, [(qc, kc, vc)], ((_UNITS_C, True, 1.0 - lam_init),),
                         n_ctx_tiles=n_ctx_tiles, name="attn_diff")

        xs, h2, top_idx, top_w = _merge(
            xs, modtab, oa, ob, oc, gates, w_o_a[l].T.astype(BF16), w_o_b[l].T.astype(BF16),
            w_o_c[l].T.astype(BF16), w_out[l].astype(BF16), norm_ffn[l].reshape(1, d),
            router_w[l].T.astype(BF16), _cols(router_b[l]), n_ctx_tiles)

        y = _moe_experts(h2.reshape(bsz * ntok, d), _moe_plan(top_idx), exp_w_gu, exp_b_gu, exp_w_down,
                         exp_b_down, l)
        last = l == depth - 1
        xs = _combine(xs, modtab, top_w, y, n_ctx_tiles, n_ctx_tiles if last else 0)
    return xs
```

```python
import functools
import math

import numpy as np
import jax
import jax.numpy as jnp
from jax import lax
from jax.experimental import pallas as pl
from jax.experimental.pallas import tpu as pltpu

F32 = jnp.float32
BF16 = jnp.bfloat16

D_MODEL = 1024
GRID_W = 64
ROPE_THETA = 10000.0
NORM_EPS = 1e-6
N_MOD = 6
A_HEADS, A_KV_HEADS, A_DIM = 8, 2, 64
B_HEADS, B_Q_RANK, B_KV_RANK, B_NOPE, B_ROPE, B_V = 8, 256, 128, 64, 32, 64
MLA_SCALE = (B_NOPE + B_ROPE) ** -0.5
LOG2E = math.log2(math.e)
C_HEADS, C_DIM, C_V = 8, 32, 64
N_EXPERTS, TOP_K = 32, 4
SWIGLU_ALPHA, SWIGLU_LIMIT = 1.702, 7.0
HEAD_V = 64
ATTN_WIDTH = 512
SUM_ROWS = 16

_IN_SIZES = (A_HEADS * A_DIM, A_KV_HEADS * A_DIM, A_KV_HEADS * A_DIM, B_Q_RANK, B_KV_RANK, B_ROPE,
             C_HEADS * 2 * C_DIM, C_HEADS * 2 * C_DIM, C_HEADS * C_V, 3 * D_MODEL)
_OFF = tuple(int(v) for v in np.cumsum((0,) + _IN_SIZES))
(_O_AQ, _O_AK, _O_AV, _O_BCQ, _O_BCKV, _O_BKR, _O_CQ, _O_CK, _O_CV, _O_GATE, _O_END) = _OFF
B_QW = B_NOPE + B_ROPE

_G_SIZES = (A_DIM, A_DIM, B_Q_RANK, B_KV_RANK, B_QW, B_QW, C_DIM, C_DIM)
_GOFF = tuple(int(v) for v in np.cumsum((0,) + _G_SIZES))
(_G_AQ, _G_AK, _G_BQA, _G_BKVA, _G_BQ, _G_BK, _G_CQ, _G_CK, _G_END) = _GOFF

LANES = 128
TM = 256
MOE_TB = 256
MOE_CW = 256
_TM_SHIFT = TM.bit_length() - 1
_K_SHIFT = TOP_K.bit_length() - 1
assert TM == 1 << _TM_SHIFT and TOP_K == 1 << _K_SHIFT
ATTN_GROUP = 16
ATTN_AHEAD = 2
ATTN_LAG = 1
ATTN_KS = 3
VMEM_LIMIT = 56 * 1024 * 1024

_NT = (((1,), (1,)), ((), ()))
_TN = (((0,), (0,)), ((), ()))


def _cparams(sem):
    return pltpu.CompilerParams(dimension_semantics=sem, vmem_limit_bytes=VMEM_LIMIT)


def _adaln_kernel(cond_ref, w_ref, b_ref, o_ref):
    cnd = cond_ref[...]
    s = (cnd * jax.nn.sigmoid(cnd)).astype(BF16)
    o_ref[0] = jnp.dot(s, w_ref[0].astype(BF16), preferred_element_type=F32) + b_ref[0]


def _adaln(cond8, w_mod, b_mod):
    depth, d, nm = w_mod.shape
    tn = 1536
    return pl.pallas_call(
        _adaln_kernel,
        grid=(depth, nm // tn),
        in_specs=[pl.BlockSpec((8, d), lambda l, j: (0, 0)),
                  pl.BlockSpec((1, d, tn), lambda l, j: (l, 0, j)),
                  pl.BlockSpec((1, 1, tn), lambda l, j: (l, 0, j))],
        out_specs=pl.BlockSpec((1, 8, tn), lambda l, j: (l, 0, j)),
        out_shape=jax.ShapeDtypeStruct((depth, 8, nm), F32),
        compiler_params=_cparams(("arbitrary", "arbitrary")),
        name="adaln",
    )(cond8, w_mod, b_mod.reshape(depth, 1, nm))


def _rms_tokens(x, gain):
    ms = jnp.mean(x * x, axis=-1, keepdims=True)
    return x * lax.rsqrt(ms + NORM_EPS) * gain


def _rms_rows(x, g):
    ms = jnp.mean(x * x, axis=0, keepdims=True)
    return x * lax.rsqrt(ms + NORM_EPS) * g


def _rope_rows(y, cos, sin):
    r = y.shape[0] // 2
    y1, y2 = y[:r], y[r:]
    return y1 * cos - y2 * sin, y2 * cos + y1 * sin


def _pre_kernel(x_ref, mod_ref, gmix_ref, w_ref, wuq_ref, wukv_ref, gcol_ref, rope_ref,
                qa_ref, ka_ref, va_ref, qb_ref, kb_ref, vb_ref, qc_ref, kc_ref, vc_ref, gates_ref):
    x = x_ref[0]
    sh1 = mod_ref[0, 0, 0:1, :]
    sc1 = mod_ref[0, 0, 1:2, :]
    h = _rms_tokens(x, gmix_ref[...]) * (1.0 + sc1) + sh1
    hb = h.astype(BF16)
    proj = lax.dot_general(w_ref[0:_O_GATE, :], hb, _NT, preferred_element_type=F32)
    gates = lax.dot_general(w_ref[_O_GATE:_O_END, :], hb, _NT, preferred_element_type=F32)
    gates_ref[0, 0] = jax.nn.sigmoid(gates).astype(BF16)

    cos64, sin64 = rope_ref[0, 0:32, :], rope_ref[0, 32:64, :]
    cos32, sin32 = rope_ref[0, 64:80, :], rope_ref[0, 80:96, :]

    def gcol(a, b):
        return gcol_ref[a:b, :]

    g_aq, g_ak = gcol(_G_AQ, _G_AK), gcol(_G_AK, _G_BQA)
    a_scale = A_DIM ** -0.5 * LOG2E
    for hd in range(A_HEADS):
        r0 = _O_AQ + hd * A_DIM
        o1, o2 = _rope_rows(_rms_rows(proj[r0:r0 + A_DIM], g_aq), cos64, sin64)
        qa_ref[0, 0, hd * A_DIM:hd * A_DIM + 32, :] = (o1 * a_scale).astype(BF16)
        qa_ref[0, 0, hd * A_DIM + 32:(hd + 1) * A_DIM, :] = (o2 * a_scale).astype(BF16)
    k_parts = []
    for g in range(A_KV_HEADS):
        r0 = _O_AK + g * A_DIM
        o1, o2 = _rope_rows(_rms_rows(proj[r0:r0 + A_DIM], g_ak), cos64, sin64)
        k_parts += [o1, o2]
    ka_ref[0] = jnp.concatenate(k_parts, axis=0).T.astype(BF16)
    va_ref[0, 0] = proj[_O_AV:_O_BCQ].astype(BF16)

    g_bq, g_bk = gcol(_G_BQ, _G_BK), gcol(_G_BK, _G_CQ)
    b_scale = MLA_SCALE * LOG2E
    cqn = _rms_rows(proj[_O_BCQ:_O_BCKV], gcol(_G_BQA, _G_BKVA)).astype(BF16)
    bq = jnp.dot(wuq_ref[...], cqn, preferred_element_type=F32)
    for hd in range(B_HEADS):
        r0 = hd * B_QW
        qn = _rms_rows(bq[r0:r0 + B_NOPE], g_bq[:B_NOPE])
        qr = _rms_rows(bq[r0 + B_NOPE:r0 + B_QW], g_bq[B_NOPE:])
        r1, r2 = _rope_rows(qr, cos32, sin32)
        qb_ref[0, 0, r0:r0 + B_NOPE, :] = (qn * b_scale).astype(BF16)
        qb_ref[0, 0, r0 + B_NOPE:r0 + B_NOPE + 16, :] = (r1 * b_scale).astype(BF16)
        qb_ref[0, 0, r0 + B_NOPE + 16:r0 + B_QW, :] = (r2 * b_scale).astype(BF16)
    ckvn = _rms_rows(proj[_O_BCKV:_O_BKR], gcol(_G_BKVA, _G_BQ)).astype(BF16)
    bkv = jnp.dot(wukv_ref[...], ckvn, preferred_element_type=F32)
    kr = _rms_rows(proj[_O_BKR:_O_CQ], g_bk[B_NOPE:])
    kr1, kr2 = _rope_rows(kr, cos32, sin32)
    zpad = jnp.zeros((LANES - B_QW, kr1.shape[1]), F32)
    k_parts = []
    for hd in range(B_HEADS):
        r0 = hd * (B_NOPE + B_V)
        kn = _rms_rows(bkv[r0:r0 + B_NOPE], g_bk[:B_NOPE])
        k_parts += [kn, kr1, kr2, zpad]
        vb_ref[0, 0, hd * B_V:(hd + 1) * B_V, :] = bkv[r0 + B_NOPE:r0 + B_NOPE + B_V].astype(BF16)
    kb_ref[0] = jnp.concatenate(k_parts, axis=0).T.astype(BF16)

    g_cq, g_ck = gcol(_G_CQ, _G_CK), gcol(_G_CK, _G_END)
    c_scale = C_DIM ** -0.5 * LOG2E
    k_parts = []
    for j in range(2 * C_HEADS):
        r0 = _O_CQ + j * C_DIM
        o1, o2 = _rope_rows(_rms_rows(proj[r0:r0 + C_DIM], g_cq), cos32, sin32)
        qc_ref[0, 0, j * C_DIM:j * C_DIM + 16, :] = (o1 * c_scale).astype(BF16)
        qc_ref[0, 0, j * C_DIM + 16:(j + 1) * C_DIM, :] = (o2 * c_scale).astype(BF16)
        r0 = _O_CK + j * C_DIM
        o1, o2 = _rope_rows(_rms_rows(proj[r0:r0 + C_DIM], g_ck), cos32, sin32)
        k_parts += [o1, o2]
    kc_ref[0] = jnp.concatenate(k_parts, axis=0).T.astype(BF16)
    vc_ref[0, 0] = proj[_O_CV:_O_GATE].astype(BF16)


def _pre_mixer(xs, modtab, gmix, w_int, wuq_t, wukv_t, gcol, rope, n_ctx_tiles):
    bsz, ntok, d = xs.shape
    nt = ntok // TM

    def fm(rows, dtype=BF16):
        return (jax.ShapeDtypeStruct((bsz, nt, rows, TM), dtype),
                pl.BlockSpec((1, 1, rows, TM), lambda b, i: (b, i, 0, 0)))

    def tmaj(cols):
        return (jax.ShapeDtypeStruct((bsz, ntok, cols), BF16),
                pl.BlockSpec((1, TM, cols), lambda b, i: (b, i, 0)))

    outs = [fm(A_HEADS * A_DIM), tmaj(LANES), fm(A_KV_HEADS * A_DIM),
            fm(B_HEADS * B_QW), tmaj(B_HEADS * LANES), fm(B_HEADS * B_V),
            fm(2 * C_HEADS * C_DIM), tmaj(2 * C_HEADS * C_DIM), fm(C_HEADS * C_V),
            fm(3 * D_MODEL)]
    const = lambda b, i: (0, 0)
    return pl.pallas_call(
        _pre_kernel,
        grid=(bsz, nt),
        in_specs=[pl.BlockSpec((1, TM, d), lambda b, i: (b, i, 0)),
                  pl.BlockSpec((1, 1, N_MOD, d), lambda b, i: (b, jnp.where(i < n_ctx_tiles, 0, 1), 0, 0)),
                  pl.BlockSpec((1, d), const),
                  pl.BlockSpec(w_int.shape, const),
                  pl.BlockSpec(wuq_t.shape, const),
                  pl.BlockSpec(wukv_t.shape, const),
                  pl.BlockSpec(gcol.shape, const),
                  pl.BlockSpec((1, rope.shape[1], TM), lambda b, i: (i, 0, 0))],
        out_specs=[o[1] for o in outs],
        out_shape=[o[0] for o in outs],
        compiler_params=_cparams(("arbitrary", "arbitrary")),
        name="pre_mixer",
    )(xs, modtab, gmix, w_int, wuq_t, wukv_t, gcol, rope)


def _attn_kernel(*refs, mixers, group_size, n_ctx_tiles, n_tiles):
    nm = len(mixers)
    lam_ref, gsub_ref = refs[0], refs[1 + 3 * nm]
    qkv = [refs[1 + 3 * m:4 + 3 * m] for m in range(nm)]
    o_refs = refs[2 + 3 * nm:2 + 4 * nm]
    s_scr = refs[2 + 4 * nm]
    i = pl.program_id(1)
    tq = qkv[0][0].shape[3]
    lat_ks = ATTN_KS if n_tiles % ATTN_KS == 0 else 1
    chains = [(m,) + u for m, (units, _, _) in enumerate(mixers) for u in units]

    def padded_q(m, q_row0, q_rows, pad_off):
        q = qkv[m][0][0, 0, q_row0:q_row0 + q_rows, :]
        parts = []
        if pad_off:
            parts.append(jnp.zeros((pad_off, tq), BF16))
        parts.append(q)
        rest = LANES - pad_off - q_rows
        if rest:
            parts.append(jnp.zeros((rest, tq), BF16))
        return jnp.concatenate(parts, axis=0) if len(parts) > 1 else q

    def run_group(group, ks, n_steps):
        qpads = [padded_q(c[0], c[1], c[2], c[3]) for c in group]
        n = len(group)
        pre = min(ATTN_AHEAD, n)
        rows = ks * TM
        ones_rows = jnp.ones((SUM_ROWS, TM), BF16)

        def qk_one(j, at_step):
            row0 = pl.multiple_of(at_step * rows, rows)
            m, k_blk = group[j][0], group[j][4]
            k = qkv[m][1][0, pl.ds(row0, rows), k_blk * LANES:(k_blk + 1) * LANES]
            return jnp.dot(k, qpads[j], preferred_element_type=F32)

        def body(step, state):
            def softmax(j, s):
                m = state[j][0]
                m_new = jnp.maximum(m, jnp.max(s, axis=0, keepdims=True))
                alpha = jnp.exp2(m - m_new)
                pb = jnp.exp2((s - m_new).astype(BF16))
                return m_new, alpha, pb

            def pv_one(j, sm):
                m_new, alpha, pb = sm
                mx, v_row0 = group[j][0], group[j][5]
                pv = None
                for t in range(ks):
                    v = qkv[mx][2][0, step * ks + t, v_row0:v_row0 + HEAD_V, :]
                    v = jnp.concatenate([v, ones_rows], axis=0)
                    d = jnp.dot(v, pb[t * TM:(t + 1) * TM], preferred_element_type=F32)
                    pv = d if pv is None else pv + d
                return m_new, alpha * state[j][1] + pv

            nxt = jnp.minimum(step + 1, n_steps - 1)
            scores, sms, new = {}, {}, [None] * n
            for j in range(n):
                s = s_scr[j, 0:rows, :] if j < pre else scores.pop(j)
                sms[j] = softmax(j, s)
                if j + pre < n:
                    scores[j + pre] = qk_one(j + pre, step)
                else:
                    s_scr[j + pre - n, 0:rows, :] = qk_one(j + pre - n, nxt)
                if j - ATTN_LAG >= 0:
                    new[j - ATTN_LAG] = pv_one(j - ATTN_LAG, sms.pop(j - ATTN_LAG))
            for j in sorted(sms):
                new[j] = pv_one(j, sms[j])
            return tuple(new)

        for j in range(pre):
            s_scr[j, 0:rows, :] = qk_one(j, 0)
        init = tuple((jnp.full((1, tq), -1e30, F32), jnp.zeros((HEAD_V + SUM_ROWS, tq), F32)) for _ in group)
        return [acc[:HEAD_V] / acc[HEAD_V:HEAD_V + 1] for (_, acc) in lax.fori_loop(0, n_steps, body, init)]

    def run_tile(ks, n_steps):
        outs = []
        for g0 in range(0, len(chains), group_size):
            outs += run_group(chains[g0:g0 + group_size], ks, n_steps)
        c0 = 0
        for m, (units, diff, out_scale) in enumerate(mixers):
            per_head = 2 if diff else 1
            for hd in range(len(units) // per_head):
                if diff:
                    o = outs[c0 + 2 * hd] - lam_ref[0] * outs[c0 + 2 * hd + 1]
                    o = _rms_rows(o, gsub_ref[...]) * out_scale
                else:
                    o = outs[c0 + hd]
                o_refs[m][0, 0, hd * HEAD_V:(hd + 1) * HEAD_V, :] = o.astype(BF16)
            c0 += len(units)

    @pl.when(i < n_ctx_tiles)
    def _():
        run_tile(1, n_ctx_tiles)

    @pl.when(i >= n_ctx_tiles)
    def _():
        run_tile(lat_ks, n_tiles // lat_ks)


def _attention(lam, gsub, qkvs, mixers, *, n_ctx_tiles, name):
    bsz, nt = qkvs[0][0].shape[:2]
    kern = functools.partial(_attn_kernel, mixers=mixers, group_size=ATTN_GROUP, n_ctx_tiles=n_ctx_tiles,
                             n_tiles=nt)
    in_specs = [pl.BlockSpec(memory_space=pltpu.SMEM)]
    args = [lam]
    for qt, k, vt in qkvs:
        in_specs += [pl.BlockSpec((1, 1, qt.shape[2], TM), lambda b, i: (b, i, 0, 0)),
                     pl.BlockSpec((1,) + k.shape[1:], lambda b, i: (b, 0, 0), pipeline_mode=pl.Buffered(1)),
                     pl.BlockSpec((1,) + vt.shape[1:], lambda b, i: (b, 0, 0, 0), pipeline_mode=pl.Buffered(1))]
        args += [qt, k, vt]
    in_specs.append(pl.BlockSpec(gsub.shape, lambda b, i: (0, 0)))
    args.append(gsub)
    out_spec = pl.BlockSpec((1, 1, ATTN_WIDTH, TM), lambda b, i: (b, i, 0, 0))
    out_shape = jax.ShapeDtypeStruct((bsz, nt, ATTN_WIDTH, TM), BF16)
    return pl.pallas_call(
        kern,
        grid=(bsz, nt),
        in_specs=in_specs,
        out_specs=[out_spec] * len(qkvs),
        out_shape=[out_shape] * len(qkvs),
        scratch_shapes=[pltpu.VMEM((ATTN_AHEAD, ATTN_KS * TM, TM), F32)],
        compiler_params=_cparams(("arbitrary", "arbitrary")),
        name=name,
    )(*args)


_UNITS_A = tuple((h * A_DIM, A_DIM, A_DIM * (h // (A_HEADS // A_KV_HEADS)), 0,
                  HEAD_V * (h // (A_HEADS // A_KV_HEADS))) for h in range(A_HEADS))
_UNITS_B = tuple((h * B_QW, B_QW, 0, h, h * B_V) for h in range(B_HEADS))
_UNITS_C = tuple((h * 2 * C_DIM + c * C_DIM, C_DIM, C_DIM * (2 * (h % 2) + c), h // 2, h * C_V)
                 for h in range(C_HEADS) for c in range(2))


def _merge_kernel(x_ref, mod_ref, oa_ref, ob_ref, oc_ref, gates_ref, woa_ref, wob_ref, woc_ref, wout_ref,
                  gffn_ref, rw_ref, rb_ref, xo_ref, h2_ref, tidx_ref, tw_ref):
    d = D_MODEL
    ya = jnp.dot(woa_ref[...], oa_ref[0, 0], preferred_element_type=F32)
    yb = jnp.dot(wob_ref[...], ob_ref[0, 0], preferred_element_type=F32)
    yc = jnp.dot(woc_ref[...], oc_ref[0, 0], preferred_element_type=F32)
    yt = (gates_ref[0, 0, 0:d, :] * ya + gates_ref[0, 0, d:2 * d, :] * yb
          + gates_ref[0, 0, 2 * d:3 * d, :] * yc)
    out = lax.dot_general(yt.astype(BF16), wout_ref[...], _TN, preferred_element_type=F32)
    g1 = mod_ref[0, 0, 2:3, :]
    x = x_ref[0] + g1 * out
    xo_ref[0] = x
    sh2, sc2 = mod_ref[0, 0, 3:4, :], mod_ref[0, 0, 4:5, :]
    h2 = _rms_tokens(x, gffn_ref[...]) * (1.0 + sc2) + sh2
    h2_ref[0] = h2
    logits = lax.dot_general(rw_ref[...], h2.astype(BF16), _NT, preferred_element_type=F32) + rb_ref[...]
    iota = lax.broadcasted_iota(jnp.int32, logits.shape, 0)
    vals, idxs = [], []
    cur = logits
    for _ in range(TOP_K):
        mx = jnp.max(cur, axis=0, keepdims=True)
        ix = jnp.min(jnp.where(cur == mx, iota, N_EXPERTS), axis=0, keepdims=True)
        vals.append(mx)
        idxs.append(ix)
        cur = jnp.where(iota == ix, -jnp.inf, cur)
    tv = jnp.concatenate(vals, axis=0)
    e = jnp.exp(tv - tv[0:1])
    tw_ref[0, 0] = e / jnp.sum(e, axis=0, keepdims=True)
    tidx_ref[0, 0] = jnp.concatenate(idxs, axis=0)


def _merge(xs, modtab, oa, ob, oc, gates, woa_t, wob_t, woc_t, wout, gffn, rw_t, rb_col, n_ctx_tiles):
    bsz, ntok, d = xs.shape
    nt = ntok // TM
    const = lambda b, i: (0, 0)
    tile4 = lambda b, i: (b, i, 0, 0)
    tok3 = lambda b, i: (b, i, 0)
    return pl.pallas_call(
        _merge_kernel,
        grid=(bsz, nt),
        in_specs=[pl.BlockSpec((1, TM, d), tok3),
                  pl.BlockSpec((1, 1, N_MOD, d), lambda b, i: (b, jnp.where(i < n_ctx_tiles, 0, 1), 0, 0)),
                  pl.BlockSpec((1, 1, ATTN_WIDTH, TM), tile4),
                  pl.BlockSpec((1, 1, ATTN_WIDTH, TM), tile4),
                  pl.BlockSpec((1, 1, ATTN_WIDTH, TM), tile4),
                  pl.BlockSpec((1, 1, 3 * d, TM), tile4),
                  pl.BlockSpec(woa_t.shape, const),
                  pl.BlockSpec(wob_t.shape, const),
                  pl.BlockSpec(woc_t.shape, const),
                  pl.BlockSpec(wout.shape, const),
                  pl.BlockSpec((1, d), const),
                  pl.BlockSpec(rw_t.shape, const),
                  pl.BlockSpec(rb_col.shape, const)],
        out_specs=[pl.BlockSpec((1, TM, d), tok3),
                   pl.BlockSpec((1, TM, d), tok3),
                   pl.BlockSpec((1, 1, TOP_K, TM), tile4),
                   pl.BlockSpec((1, 1, TOP_K, TM), tile4)],
        out_shape=[jax.ShapeDtypeStruct((bsz, ntok, d), F32),
                   jax.ShapeDtypeStruct((bsz, ntok, d), F32),
                   jax.ShapeDtypeStruct((bsz, nt, TOP_K, TM), jnp.int32),
                   jax.ShapeDtypeStruct((bsz, nt, TOP_K, TM), F32)],
        compiler_params=_cparams(("arbitrary", "arbitrary")),
        name="merge_router",
    )(xs, modtab, oa, ob, oc, gates, woa_t, wob_t, woc_t, wout, gffn, rw_t, rb_col)


def _moe_kernel(ib_ref, ie_ref, lo_ref, hi_ref, first_ref, nused_ref,
                src_cur_ref, dst_cur_ref, src_next_ref,
                x_hbm, wgu_ref, bgu_ref, wdn_ref, bdn_ref, y_hbm,
                xbuf0, xbuf1, ybuf0, ybuf1, wgu_bf, wdn_bf, gsem, ssem, *, spare_row0):
    i = pl.program_id(0)
    n_used = nused_ref[0]
    d = D_MODEL

    def gather_row(src_ref, r, xb, sem):
        return pltpu.make_async_copy(x_hbm.at[pl.ds(src_ref[0, 0, r], 1), :], xb.at[pl.ds(r, 1), :], sem)

    def scatter_row(r, lo, hi, yb, sem, spare):
        row = jnp.where((lo <= r) & (r < hi), dst_cur_ref[0, 0, r], spare + r)
        return pltpu.make_async_copy(yb.at[pl.ds(r, 1), :], y_hbm.at[pl.ds(row, 1), :], sem)

    def wait_gather(xb, sem):
        pltpu.make_async_copy(x_hbm.at[pl.ds(0, MOE_TB), :], xb, sem).wait()

    def wait_scatter(yb, sem):
        pltpu.make_async_copy(yb, y_hbm.at[pl.ds(0, MOE_TB), :], sem).wait()

    def step(p):
        xb_cur, xb_nxt = (xbuf0, xbuf1) if p == 0 else (xbuf1, xbuf0)
        yb_cur, yb_prv = (ybuf0, ybuf1) if p == 0 else (ybuf1, ybuf0)
        q = 1 - p

        if p == 0:
            @pl.when(i == 0)
            def _():
                for r in range(MOE_TB):
                    gather_row(src_cur_ref, r, xb_cur, gsem.at[p]).start(priority=r % 2)
                yb_prv[...] = jnp.zeros_like(yb_prv)
                for half in range(2):
                    spare = pltpu.make_async_copy(
                        yb_prv, y_hbm.at[pl.ds(spare_row0 + half * MOE_TB, MOE_TB), :], ssem.at[q])
                    spare.start()
                    spare.wait()

        @pl.when(first_ref[i] == 1)
        def _():
            wgu_bf[...] = wgu_ref[0, 0].astype(BF16)
            wdn_bf[...] = wdn_ref[0, 0].astype(BF16)

        for r in range(MOE_TB):
            gather_row(src_next_ref, r, xb_nxt, gsem.at[q]).start(priority=r % 2)
        wait_gather(xb_cur, gsem.at[p])

        xb = xb_cur[...].astype(BF16)
        y = None
        for c in range(d // MOE_CW):
            c0, c1 = c * MOE_CW, (c + 1) * MOE_CW
            glu = jnp.dot(xb, wgu_bf[:, c0:c1], preferred_element_type=F32) + bgu_ref[0, 0, :, c0:c1]
            lin = jnp.dot(xb, wgu_bf[:, d + c0:d + c1], preferred_element_type=F32) + bgu_ref[0, 0, :, d + c0:d + c1]
            glu = jnp.minimum(glu, SWIGLU_LIMIT)
            lin = jnp.clip(lin, -SWIGLU_LIMIT, SWIGLU_LIMIT)
            act = glu * jax.nn.sigmoid(SWIGLU_ALPHA * glu) * (lin + 1.0)
            part = jnp.dot(act.astype(BF16), wdn_bf[c0:c1, :], preferred_element_type=F32)
            y = part if y is None else y + part
        y = y + bdn_ref[0, 0]

        @pl.when(i >= 2)
        def _():
            wait_scatter(yb_cur, ssem.at[p])

        yb_cur[...] = y
        lo_cur, hi_cur = lo_ref[i], hi_ref[i]
        for r in range(MOE_TB):
            scatter_row(r, lo_cur, hi_cur, yb_cur, ssem.at[p], spare_row0 + p * MOE_TB).start(priority=r % 2)

        @pl.when(i == n_used - 1)
        def _():
            @pl.when(i >= 1)
            def _():
                wait_scatter(yb_prv, ssem.at[q])
            wait_scatter(yb_cur, ssem.at[p])
            wait_gather(xb_nxt, gsem.at[q])

    for p in range(2):
        @pl.when((i < n_used) & (i % 2 == p))
        def _(p=p):
            step(p)


def _moe_experts(h2_flat, plan, w_gu, b_gu, w_dn, b_dn, layer):
    item_b, item_e, lo, hi, first, n_used, src_sorted, dst_sorted = plan
    n_items = item_b.shape[0]
    n_tok, d = h2_flat.shape
    nb = src_sorted.shape[0] // MOE_TB

    def idx_block(f):
        return pl.BlockSpec((1, 1, MOE_TB), lambda i, ib, ie, lo_, hi_, fs, nu: (ib[f(i)], 0, 0),
                            memory_space=pltpu.SMEM)
    wmap = lambda i, ib, ie, lo_, hi_, fs, nu: (layer, ie[i], 0, 0)
    grid_spec = pltpu.PrefetchScalarGridSpec(
        num_scalar_prefetch=6,
        grid=(n_items,),
        in_specs=[idx_block(lambda i: i),
                  idx_block(lambda i: i),
                  idx_block(lambda i: jnp.minimum(i + 1, n_items - 1)),
                  pl.BlockSpec(memory_space=pl.ANY),
                  pl.BlockSpec((1, 1, d, 2 * d), wmap),
                  pl.BlockSpec((1, 1, 1, 2 * d), wmap),
                  pl.BlockSpec((1, 1, d, d), wmap),
                  pl.BlockSpec((1, 1, 1, d), wmap)],
        out_specs=pl.BlockSpec(memory_space=pl.ANY),
        scratch_shapes=[pltpu.VMEM((MOE_TB, d), F32), pltpu.VMEM((MOE_TB, d), F32),
                        pltpu.VMEM((MOE_TB, d), F32), pltpu.VMEM((MOE_TB, d), F32),
                        pltpu.VMEM((d, 2 * d), BF16), pltpu.VMEM((d, d), BF16),
                        pltpu.SemaphoreType.DMA((2,)), pltpu.SemaphoreType.DMA((2,))],
    )
    depth, ne = b_gu.shape[0], b_gu.shape[1]
    src3 = src_sorted.reshape(nb, 1, MOE_TB)
    dst3 = dst_sorted.reshape(nb, 1, MOE_TB)
    return pl.pallas_call(
        functools.partial(_moe_kernel, spare_row0=TOP_K * n_tok),
        grid_spec=grid_spec,
        out_shape=jax.ShapeDtypeStruct((TOP_K * n_tok + 2 * MOE_TB, d), F32),
        compiler_params=pltpu.CompilerParams(dimension_semantics=("arbitrary",), vmem_limit_bytes=VMEM_LIMIT,
                                             has_side_effects=True),
        name="moe_experts",
    )(item_b, item_e, lo, hi, first, n_used, src3, dst3, src3, h2_flat,
      w_gu, b_gu.reshape(depth, ne, 1, 2 * d), w_dn, b_dn.reshape(depth, ne, 1, d))


def _moe_plan(top_idx):
    i32 = jnp.int32
    flat_e = top_idx.reshape(-1)
    n_assign = flat_e.shape[0]
    assert n_assign % MOE_TB == 0
    nb = n_assign // MOE_TB
    n_items = nb + N_EXPERTS
    order = jnp.argsort(flat_e, stable=True).astype(i32)
    counts = jnp.sum((flat_e[None, :] == jnp.arange(N_EXPERTS, dtype=i32)[:, None]).astype(i32), axis=1)
    ends = jnp.cumsum(counts)
    starts = ends - counts
    pos = jnp.sort(jnp.concatenate([jnp.arange(nb, dtype=i32) * MOE_TB, starts]))
    length = jnp.concatenate([pos[1:], jnp.full((1,), n_assign, i32)]) - pos
    keep = length > 0
    perm = jnp.argsort(jnp.where(keep, pos, n_assign + jnp.arange(n_items, dtype=i32)))
    pos, length, keep = pos[perm], length[perm], keep[perm]
    item_b = jnp.minimum(pos // MOE_TB, nb - 1)
    item_e = jnp.minimum(jnp.sum((pos[:, None] >= ends[None, :]).astype(i32), axis=1), N_EXPERTS - 1)
    lo = pos - item_b * MOE_TB
    hi = lo + length
    prev_e = jnp.concatenate([jnp.full((1,), -1, i32), item_e[:-1]])
    first = (keep & (item_e != prev_e)).astype(i32)
    n_used = jnp.sum(keep.astype(i32)).reshape(1)
    n_tok = n_assign // TOP_K
    k = lax.shift_right_logical(order, _TM_SHIFT) & (TOP_K - 1)
    tok = lax.shift_left(lax.shift_right_logical(order, _TM_SHIFT + _K_SHIFT), _TM_SHIFT) | (order & (TM - 1))
    return (item_b.astype(i32), item_e.astype(i32), lo.astype(i32), hi.astype(i32), first, n_used,
            tok.astype(i32), (k * n_tok + tok).astype(i32))


def _combine_kernel(x_ref, mod_ref, tw_ref, y0_ref, y1_ref, y2_ref, y3_ref, o_ref):
    tm = x_ref.shape[1]
    w8 = jnp.concatenate([tw_ref[0, 0], jnp.zeros((8 - TOP_K, tm), F32)], axis=0)
    wt = w8.T
    tot = (wt[:, 0:1] * y0_ref[...] + wt[:, 1:2] * y1_ref[...]) + (wt[:, 2:3] * y2_ref[...] + wt[:, 3:4] * y3_ref[...])
    o_ref[0] = x_ref[0] + mod_ref[0, 0, 5:6, :] * tot


def _combine(xs, modtab, top_w, y, n_ctx_tiles, skip_tiles):
    bsz, ntok, d = xs.shape
    nt = ntok // TM
    n_out = nt - skip_tiles
    yrow = lambda k: pl.BlockSpec((TM, d), lambda b, i: ((k * bsz + b) * nt + i + skip_tiles, 0))
    return pl.pallas_call(
        _combine_kernel,
        grid=(bsz, n_out),
        in_specs=[pl.BlockSpec((1, TM, d), lambda b, i: (b, i + skip_tiles, 0)),
                  pl.BlockSpec((1, 1, N_MOD, d),
                               lambda b, i: (b, jnp.where(i + skip_tiles < n_ctx_tiles, 0, 1), 0, 0)),
                  pl.BlockSpec((1, 1, TOP_K, TM), lambda b, i: (b, i + skip_tiles, 0, 0)),
                  yrow(0), yrow(1), yrow(2), yrow(3)],
        out_specs=pl.BlockSpec((1, TM, d), lambda b, i: (b, i, 0)),
        out_shape=jax.ShapeDtypeStruct((bsz, n_out * TM, d), F32),
        compiler_params=_cparams(("arbitrary", "arbitrary")),
        name="moe_combine",
    )(xs, modtab, top_w, y, y, y, y)


def _rope_table(n_ctx, n_lat):
    n_rows = n_lat // GRID_W
    row = jnp.repeat(jnp.arange(n_rows, dtype=F32), GRID_W)
    col = jnp.tile(jnp.arange(GRID_W, dtype=F32), n_rows)

    def table(rot_dim):
        axis_pairs = rot_dim // 4
        inv_freq = ROPE_THETA ** (-jnp.arange(axis_pairs, dtype=F32) / axis_pairs)
        ang = jnp.concatenate([row[:, None] * inv_freq, col[:, None] * inv_freq], axis=-1)
        cos = jnp.concatenate([jnp.ones((n_ctx, 2 * axis_pairs), F32), jnp.cos(ang)], axis=0)
        sin = jnp.concatenate([jnp.zeros((n_ctx, 2 * axis_pairs), F32), jnp.sin(ang)], axis=0)
        return cos, sin

    c64, s64 = table(A_DIM)
    c32, s32 = table(B_ROPE)
    tab = jnp.concatenate([c64, s64, c32, s32], axis=-1)
    ntok = n_ctx + n_lat
    return jnp.transpose(tab.reshape(ntok // TM, TM, tab.shape[1]), (0, 2, 1))


def _cols(v):
    return jnp.broadcast_to(v.astype(F32)[:, None], (v.shape[0], TM))


def kernel(x, c, ctx, c_ctx, w_mod, b_mod, norm_mix, norm_ffn, w_in, a_q_norm, a_k_norm, b_q_a_norm,
           b_kv_a_norm, b_w_uq, b_w_ukv, b_q_norm, b_k_norm, c_q_norm, c_k_norm, c_lambda, c_subln,
           w_o_a, w_o_b, w_o_c, w_out, router_w, router_b, exp_w_gu, exp_b_gu, exp_w_down, exp_b_down):
    bsz, n_lat, d = x.shape
    n_ctx = ctx.shape[1]
    depth = w_mod.shape[0]
    assert d == D_MODEL and n_ctx % TM == 0 and n_lat % TM == 0 and n_lat % GRID_W == 0 and bsz + 1 <= 8
    n_ctx_tiles = n_ctx // TM
    ntok = n_ctx + n_lat

    xs = jnp.concatenate([ctx, x], axis=1)
    cond8 = jnp.zeros((8, d), F32).at[:bsz].set(c).at[bsz].set(c_ctx)
    mod_all = _adaln(cond8, w_mod, b_mod).reshape(depth, 8, N_MOD, d)
    rope = _rope_table(n_ctx, n_lat)

    for l in range(depth):
        lam_init = 0.8 - 0.6 * math.exp(-0.3 * l)
        lq1, lk1, lq2, lk2 = c_lambda[l].astype(F32)
        lam = (jnp.exp(jnp.sum(lq1 * lk1)) - jnp.exp(jnp.sum(lq2 * lk2)) + lam_init).reshape(1)
        modtab = jnp.stack([jnp.broadcast_to(mod_all[l, bsz], (bsz, N_MOD, d)), mod_all[l, :bsz]], axis=1)
        gcol = jnp.concatenate([_cols(a_q_norm[l]), _cols(a_k_norm[l]), _cols(b_q_a_norm[l]),
                                _cols(b_kv_a_norm[l]), _cols(b_q_norm[l]), _cols(b_k_norm[l]),
                                _cols(c_q_norm[l]), _cols(c_k_norm[l])], axis=0)
        (qa, ka, va, qb, kb, vb, qc, kc, vc, gates) = _pre_mixer(
            xs, modtab, norm_mix[l].reshape(1, d), w_in[l].T.astype(BF16), b_w_uq[l].T.astype(BF16),
            b_w_ukv[l].T.astype(BF16), gcol, rope, n_ctx_tiles)

        gsub = _cols(c_subln[l])
        oa, ob = _attention(lam, gsub, [(qa, ka, va), (qb, kb, vb)],
                            ((_UNITS_A, False, 1.0), (_UNITS_B, False, 1.0)),
                            n_ctx_tiles=n_ctx_tiles, name="attn_gqa_mla")
        oc, = _attention(lam, gsub, [(qc, kc, vc)], ((_UNITS_C, True, 1.0 - lam_init),),
                         n_ctx_tiles=n_ctx_tiles, name="attn_diff")

        xs, h2, top_idx, top_w = _merge(
            xs, modtab, oa, ob, oc, gates, w_o_a[l].T.astype(BF16), w_o_b[l].T.astype(BF16),
            w_o_c[l].T.astype(BF16), w_out[l].astype(BF16), norm_ffn[l].reshape(1, d),
            router_w[l].T.astype(BF16), _cols(router_b[l]), n_ctx_tiles)

        y = _moe_experts(h2.reshape(bsz * ntok, d), _moe_plan(top_idx), exp_w_gu, exp_b_gu, exp_w_down,
                         exp_b_down, l)
        last = l == depth - 1
        xs = _combine(xs, modtab, top_w, y, n_ctx_tiles, n_ctx_tiles if last else 0)
    return xs
```

```python
import functools
import math

import numpy as np
import jax
import jax.numpy as jnp
from jax import lax
from jax.experimental import pallas as pl
from jax.experimental.pallas import tpu as pltpu

F32 = jnp.float32
BF16 = jnp.bfloat16

D_MODEL = 1024
GRID_W = 64
ROPE_THETA = 10000.0
NORM_EPS = 1e-6
N_MOD = 6
A_HEADS, A_KV_HEADS, A_DIM = 8, 2, 64
B_HEADS, B_Q_RANK, B_KV_RANK, B_NOPE, B_ROPE, B_V = 8, 256, 128, 64, 32, 64
MLA_SCALE = (B_NOPE + B_ROPE) ** -0.5
LOG2E = math.log2(math.e)
C_HEADS, C_DIM, C_V = 8, 32, 64
N_EXPERTS, TOP_K = 32, 4
SWIGLU_ALPHA, SWIGLU_LIMIT = 1.702, 7.0
HEAD_V = 64
ATTN_WIDTH = 512

_IN_SIZES = (A_HEADS * A_DIM, A_KV_HEADS * A_DIM, A_KV_HEADS * A_DIM, B_Q_RANK, B_KV_RANK, B_ROPE,
             C_HEADS * 2 * C_DIM, C_HEADS * 2 * C_DIM, C_HEADS * C_V, 3 * D_MODEL)
_OFF = tuple(int(v) for v in np.cumsum((0,) + _IN_SIZES))
(_O_AQ, _O_AK, _O_AV, _O_BCQ, _O_BCKV, _O_BKR, _O_CQ, _O_CK, _O_CV, _O_GATE, _O_END) = _OFF
B_QW = B_NOPE + B_ROPE

_G_SIZES = (A_DIM, A_DIM, B_Q_RANK, B_KV_RANK, B_QW, B_QW, C_DIM, C_DIM)
_GOFF = tuple(int(v) for v in np.cumsum((0,) + _G_SIZES))
(_G_AQ, _G_AK, _G_BQA, _G_BKVA, _G_BQ, _G_BK, _G_CQ, _G_CK, _G_END) = _GOFF

LANES = 128
TM = 256
MOE_TB = 256
MOE_CW = 1024
_TM_SHIFT = TM.bit_length() - 1
_K_SHIFT = TOP_K.bit_length() - 1
assert TM == 1 << _TM_SHIFT and TOP_K == 1 << _K_SHIFT
ATTN_GROUP = 16
ATTN_AHEAD = 2
ATTN_LAG = 1
ATTN_KS = 3
VMEM_LIMIT = 56 * 1024 * 1024

_NT = (((1,), (1,)), ((), ()))
_TN = (((0,), (0,)), ((), ()))


def _cparams(sem):
    return pltpu.CompilerParams(dimension_semantics=sem, vmem_limit_bytes=VMEM_LIMIT)


def _adaln_kernel(cond_ref, w_ref, b_ref, o_ref):
    cnd = cond_ref[...]
    s = (cnd * jax.nn.sigmoid(cnd)).astype(BF16)
    o_ref[0] = jnp.dot(s, w_ref[0].astype(BF16), preferred_element_type=F32) + b_ref[0]


def _adaln(cond8, w_mod, b_mod):
    depth, d, nm = w_mod.shape
    tn = 1536
    return pl.pallas_call(
        _adaln_kernel,
        grid=(depth, nm // tn),
        in_specs=[pl.BlockSpec((8, d), lambda l, j: (0, 0)),
                  pl.BlockSpec((1, d, tn), lambda l, j: (l, 0, j)),
                  pl.BlockSpec((1, 1, tn), lambda l, j: (l, 0, j))],
        out_specs=pl.BlockSpec((1, 8, tn), lambda l, j: (l, 0, j)),
        out_shape=jax.ShapeDtypeStruct((depth, 8, nm), F32),
        compiler_params=_cparams(("arbitrary", "arbitrary")),
        name="adaln",
    )(cond8, w_mod, b_mod.reshape(depth, 1, nm))


def _rms_tokens(x, gain):
    ms = jnp.mean(x * x, axis=-1, keepdims=True)
    return x * lax.rsqrt(ms + NORM_EPS) * gain


def _rms_rows(x, g):
    ms = jnp.mean(x * x, axis=0, keepdims=True)
    return x * lax.rsqrt(ms + NORM_EPS) * g


def _rope_rows(y, cos, sin):
    r = y.shape[0] // 2
    y1, y2 = y[:r], y[r:]
    return y1 * cos - y2 * sin, y2 * cos + y1 * sin


def _pre_kernel(x_ref, mod_ref, gmix_ref, w_ref, wuq_ref, wukv_ref, gcol_ref, rope_ref,
                qa_ref, ka_ref, va_ref, qb_ref, kb_ref, vb_ref, qc_ref, kc_ref, vc_ref, gates_ref):
    x = x_ref[0]
    sh1 = mod_ref[0, 0, 0:1, :]
    sc1 = mod_ref[0, 0, 1:2, :]
    h = _rms_tokens(x, gmix_ref[...]) * (1.0 + sc1) + sh1
    hb = h.astype(BF16)
    proj = lax.dot_general(w_ref[0:_O_GATE, :], hb, _NT, preferred_element_type=F32)
    gates = lax.dot_general(w_ref[_O_GATE:_O_END, :], hb, _NT, preferred_element_type=F32)
    gates_ref[0, 0] = jax.nn.sigmoid(gates).astype(BF16)

    cos64, sin64 = rope_ref[0, 0:32, :], rope_ref[0, 32:64, :]
    cos32, sin32 = rope_ref[0, 64:80, :], rope_ref[0, 80:96, :]

    def gcol(a, b):
        return gcol_ref[a:b, :]

    g_aq, g_ak = gcol(_G_AQ, _G_AK), gcol(_G_AK, _G_BQA)
    a_scale = A_DIM ** -0.5 * LOG2E
    for hd in range(A_HEADS):
        r0 = _O_AQ + hd * A_DIM
        o1, o2 = _rope_rows(_rms_rows(proj[r0:r0 + A_DIM], g_aq), cos64, sin64)
        qa_ref[0, 0, hd * A_DIM:hd * A_DIM + 32, :] = (o1 * a_scale).astype(BF16)
        qa_ref[0, 0, hd * A_DIM + 32:(hd + 1) * A_DIM, :] = (o2 * a_scale).astype(BF16)
    k_parts = []
    for g in range(A_KV_HEADS):
        r0 = _O_AK + g * A_DIM
        o1, o2 = _rope_rows(_rms_rows(proj[r0:r0 + A_DIM], g_ak), cos64, sin64)
        k_parts += [o1, o2]
    ka_ref[0] = jnp.concatenate(k_parts, axis=0).T.astype(BF16)
    va_ref[0, 0] = proj[_O_AV:_O_BCQ].astype(BF16)

    g_bq, g_bk = gcol(_G_BQ, _G_BK), gcol(_G_BK, _G_CQ)
    b_scale = MLA_SCALE * LOG2E
    cqn = _rms_rows(proj[_O_BCQ:_O_BCKV], gcol(_G_BQA, _G_BKVA)).astype(BF16)
    bq = jnp.dot(wuq_ref[...], cqn, preferred_element_type=F32)
    for hd in range(B_HEADS):
        r0 = hd * B_QW
        qn = _rms_rows(bq[r0:r0 + B_NOPE], g_bq[:B_NOPE])
        qr = _rms_rows(bq[r0 + B_NOPE:r0 + B_QW], g_bq[B_NOPE:])
        r1, r2 = _rope_rows(qr, cos32, sin32)
        qb_ref[0, 0, r0:r0 + B_NOPE, :] = (qn * b_scale).astype(BF16)
        qb_ref[0, 0, r0 + B_NOPE:r0 + B_NOPE + 16, :] = (r1 * b_scale).astype(BF16)
        qb_ref[0, 0, r0 + B_NOPE + 16:r0 + B_QW, :] = (r2 * b_scale).astype(BF16)
    ckvn = _rms_rows(proj[_O_BCKV:_O_BKR], gcol(_G_BKVA, _G_BQ)).astype(BF16)
    bkv = jnp.dot(wukv_ref[...], ckvn, preferred_element_type=F32)
    kr = _rms_rows(proj[_O_BKR:_O_CQ], g_bk[B_NOPE:])
    kr1, kr2 = _rope_rows(kr, cos32, sin32)
    zpad = jnp.zeros((LANES - B_QW, kr1.shape[1]), F32)
    k_parts = []
    for hd in range(B_HEADS):
        r0 = hd * (B_NOPE + B_V)
        kn = _rms_rows(bkv[r0:r0 + B_NOPE], g_bk[:B_NOPE])
        k_parts += [kn, kr1, kr2, zpad]
        vb_ref[0, 0, hd * B_V:(hd + 1) * B_V, :] = bkv[r0 + B_NOPE:r0 + B_NOPE + B_V].astype(BF16)
    kb_ref[0] = jnp.concatenate(k_parts, axis=0).T.astype(BF16)

    g_cq, g_ck = gcol(_G_CQ, _G_CK), gcol(_G_CK, _G_END)
    c_scale = C_DIM ** -0.5 * LOG2E
    k_parts = []
    for j in range(2 * C_HEADS):
        r0 = _O_CQ + j * C_DIM
        o1, o2 = _rope_rows(_rms_rows(proj[r0:r0 + C_DIM], g_cq), cos32, sin32)
        qc_ref[0, 0, j * C_DIM:j * C_DIM + 16, :] = (o1 * c_scale).astype(BF16)
        qc_ref[0, 0, j * C_DIM + 16:(j + 1) * C_DIM, :] = (o2 * c_scale).astype(BF16)
        r0 = _O_CK + j * C_DIM
        o1, o2 = _rope_rows(_rms_rows(proj[r0:r0 + C_DIM], g_ck), cos32, sin32)
        k_parts += [o1, o2]
    kc_ref[0] = jnp.concatenate(k_parts, axis=0).T.astype(BF16)
    vc_ref[0, 0] = proj[_O_CV:_O_GATE].astype(BF16)


def _pre_mixer(xs, modtab, gmix, w_int, wuq_t, wukv_t, gcol, rope, n_ctx_tiles):
    bsz, ntok, d = xs.shape
    nt = ntok // TM

    def fm(rows, dtype=BF16):
        return (jax.ShapeDtypeStruct((bsz, nt, rows, TM), dtype),
                pl.BlockSpec((1, 1, rows, TM), lambda b, i: (b, i, 0, 0)))

    def tmaj(cols):
        return (jax.ShapeDtypeStruct((bsz, ntok, cols), BF16),
                pl.BlockSpec((1, TM, cols), lambda b, i: (b, i, 0)))

    outs = [fm(A_HEADS * A_DIM), tmaj(LANES), fm(A_KV_HEADS * A_DIM),
            fm(B_HEADS * B_QW), tmaj(B_HEADS * LANES), fm(B_HEADS * B_V),
            fm(2 * C_HEADS * C_DIM), tmaj(2 * C_HEADS * C_DIM), fm(C_HEADS * C_V),
            fm(3 * D_MODEL)]
    const = lambda b, i: (0, 0)
    return pl.pallas_call(
        _pre_kernel,
        grid=(bsz, nt),
        in_specs=[pl.BlockSpec((1, TM, d), lambda b, i: (b, i, 0)),
                  pl.BlockSpec((1, 1, N_MOD, d), lambda b, i: (b, jnp.where(i < n_ctx_tiles, 0, 1), 0, 0)),
                  pl.BlockSpec((1, d), const),
                  pl.BlockSpec(w_int.shape, const),
                  pl.BlockSpec(wuq_t.shape, const),
                  pl.BlockSpec(wukv_t.shape, const),
                  pl.BlockSpec(gcol.shape, const),
                  pl.BlockSpec((1, rope.shape[1], TM), lambda b, i: (i, 0, 0))],
        out_specs=[o[1] for o in outs],
        out_shape=[o[0] for o in outs],
        compiler_params=_cparams(("arbitrary", "arbitrary")),
        name="pre_mixer",
    )(xs, modtab, gmix, w_int, wuq_t, wukv_t, gcol, rope)


def _attn_kernel(*refs, mixers, group_size, n_ctx_tiles, n_tiles):
    nm = len(mixers)
    lam_ref, gsub_ref = refs[0], refs[1 + 3 * nm]
    qkv = [refs[1 + 3 * m:4 + 3 * m] for m in range(nm)]
    o_refs = refs[2 + 3 * nm:2 + 4 * nm]
    s_scr = refs[2 + 4 * nm]
    i = pl.program_id(1)
    tq = qkv[0][0].shape[3]
    lat_ks = ATTN_KS if n_tiles % ATTN_KS == 0 else 1
    chains = [(m,) + u for m, (units, _, _) in enumerate(mixers) for u in units]

    def padded_q(m, q_row0, q_rows, pad_off):
        q = qkv[m][0][0, 0, q_row0:q_row0 + q_rows, :]
        parts = []
        if pad_off:
            parts.append(jnp.zeros((pad_off, tq), BF16))
        parts.append(q)
        rest = LANES - pad_off - q_rows
        if rest:
            parts.append(jnp.zeros((rest, tq), BF16))
        return jnp.concatenate(parts, axis=0) if len(parts) > 1 else q

    def run_group(group, ks, n_steps):
        qpads = [padded_q(c[0], c[1], c[2], c[3]) for c in group]
        n = len(group)
        pre = min(ATTN_AHEAD, n)
        rows = ks * TM

        def qk_one(j, at_step):
            row0 = pl.multiple_of(at_step * rows, rows)
            m, k_blk = group[j][0], group[j][4]
            k = qkv[m][1][0, pl.ds(row0, rows), k_blk * LANES:(k_blk + 1) * LANES]
            return jnp.dot(k, qpads[j], preferred_element_type=F32)

        def body(step, state):
            def softmax(j, s):
                m, l, _ = state[j]
                m_new = jnp.maximum(m, jnp.max(s, axis=0, keepdims=True))
                alpha = jnp.exp2(m - m_new)
                p = jnp.exp2(s - m_new)
                l = alpha * l + jnp.sum(p, axis=0, keepdims=True)
                return m_new, l, alpha, p.astype(BF16)

            def pv_one(j, sm):
                m_new, l, alpha, pb = sm
                mx, v_row0 = group[j][0], group[j][5]
                pv = None
                for t in range(ks):
                    v = qkv[mx][2][0, step * ks + t, v_row0:v_row0 + HEAD_V, :]
                    d = jnp.dot(v, pb[t * TM:(t + 1) * TM], preferred_element_type=F32)
                    pv = d if pv is None else pv + d
                return m_new, l, alpha * state[j][2] + pv

            nxt = jnp.minimum(step + 1, n_steps - 1)
            scores, sms, new = {}, {}, [None] * n
            for j in range(n):
                s = s_scr[j, 0:rows, :] if j < pre else scores.pop(j)
                sms[j] = softmax(j, s)
                if j + pre < n:
                    scores[j + pre] = qk_one(j + pre, step)
                else:
                    s_scr[j + pre - n, 0:rows, :] = qk_one(j + pre - n, nxt)
                if j - ATTN_LAG >= 0:
                    new[j - ATTN_LAG] = pv_one(j - ATTN_LAG, sms.pop(j - ATTN_LAG))
            for j in sorted(sms):
                new[j] = pv_one(j, sms[j])
            return tuple(new)

        for j in range(pre):
            s_scr[j, 0:rows, :] = qk_one(j, 0)
        init = tuple((jnp.full((1, tq), -1e30, F32), jnp.zeros((1, tq), F32), jnp.zeros((HEAD_V, tq), F32))
                     for _ in group)
        return [acc / l for (_, l, acc) in lax.fori_loop(0, n_steps, body, init)]

    def run_tile(ks, n_steps):
        outs = []
        for g0 in range(0, len(chains), group_size):
            outs += run_group(chains[g0:g0 + group_size], ks, n_steps)
        c0 = 0
        for m, (units, diff, out_scale) in enumerate(mixers):
            per_head = 2 if diff else 1
            for hd in range(len(units) // per_head):
                if diff:
                    o = outs[c0 + 2 * hd] - lam_ref[0] * outs[c0 + 2 * hd + 1]
                    o = _rms_rows(o, gsub_ref[...]) * out_scale
                else:
                    o = outs[c0 + hd]
                o_refs[m][0, 0, hd * HEAD_V:(hd + 1) * HEAD_V, :] = o.astype(BF16)
            c0 += len(units)

    @pl.when(i < n_ctx_tiles)
    def _():
        run_tile(1, n_ctx_tiles)

    @pl.when(i >= n_ctx_tiles)
    def _():
        run_tile(lat_ks, n_tiles // lat_ks)


def _attention(lam, gsub, qkvs, mixers, *, n_ctx_tiles, name):
    bsz, nt = qkvs[0][0].shape[:2]
    kern = functools.partial(_attn_kernel, mixers=mixers, group_size=ATTN_GROUP, n_ctx_tiles=n_ctx_tiles,
                             n_tiles=nt)
    in_specs = [pl.BlockSpec(memory_space=pltpu.SMEM)]
    args = [lam]
    for qt, k, vt in qkvs:
        in_specs += [pl.BlockSpec((1, 1, qt.shape[2], TM), lambda b, i: (b, i, 0, 0)),
                     pl.BlockSpec((1,) + k.shape[1:], lambda b, i: (b, 0, 0), pipeline_mode=pl.Buffered(1)),
                     pl.BlockSpec((1,) + vt.shape[1:], lambda b, i: (b, 0, 0, 0), pipeline_mode=pl.Buffered(1))]
        args += [qt, k, vt]
    in_specs.append(pl.BlockSpec(gsub.shape, lambda b, i: (0, 0)))
    args.append(gsub)
    out_spec = pl.BlockSpec((1, 1, ATTN_WIDTH, TM), lambda b, i: (b, i, 0, 0))
    out_shape = jax.ShapeDtypeStruct((bsz, nt, ATTN_WIDTH, TM), BF16)
    return pl.pallas_call(
        kern,
        grid=(bsz, nt),
        in_specs=in_specs,
        out_specs=[out_spec] * len(qkvs),
        out_shape=[out_shape] * len(qkvs),
        scratch_shapes=[pltpu.VMEM((ATTN_AHEAD, ATTN_KS * TM, TM), F32)],
        compiler_params=_cparams(("arbitrary", "arbitrary")),
        name=name,
    )(*args)


_UNITS_A = tuple((h * A_DIM, A_DIM, A_DIM * (h // (A_HEADS // A_KV_HEADS)), 0,
                  HEAD_V * (h // (A_HEADS // A_KV_HEADS))) for h in range(A_HEADS))
_UNITS_B = tuple((h * B_QW, B_QW, 0, h, h * B_V) for h in range(B_HEADS))
_UNITS_C = tuple((h * 2 * C_DIM + c * C_DIM, C_DIM, C_DIM * (2 * (h % 2) + c), h // 2, h * C_V)
                 for h in range(C_HEADS) for c in range(2))


def _merge_kernel(x_ref, mod_ref, oa_ref, ob_ref, oc_ref, gates_ref, woa_ref, wob_ref, woc_ref, wout_ref,
                  gffn_ref, rw_ref, rb_ref, xo_ref, h2_ref, tidx_ref, tw_ref):
    d = D_MODEL
    ya = jnp.dot(woa_ref[...], oa_ref[0, 0], preferred_element_type=F32)
    yb = jnp.dot(wob_ref[...], ob_ref[0, 0], preferred_element_type=F32)
    yc = jnp.dot(woc_ref[...], oc_ref[0, 0], preferred_element_type=F32)
    yt = (gates_ref[0, 0, 0:d, :] * ya + gates_ref[0, 0, d:2 * d, :] * yb
          + gates_ref[0, 0, 2 * d:3 * d, :] * yc)
    out = lax.dot_general(yt.astype(BF16), wout_ref[...], _TN, preferred_element_type=F32)
    g1 = mod_ref[0, 0, 2:3, :]
    x = x_ref[0] + g1 * out
    xo_ref[0] = x
    sh2, sc2 = mod_ref[0, 0, 3:4, :], mod_ref[0, 0, 4:5, :]
    h2 = _rms_tokens(x, gffn_ref[...]) * (1.0 + sc2) + sh2
    h2_ref[0] = h2
    logits = lax.dot_general(rw_ref[...], h2.astype(BF16), _NT, preferred_element_type=F32) + rb_ref[...]
    iota = lax.broadcasted_iota(jnp.int32, logits.shape, 0)
    vals, idxs = [], []
    cur = logits
    for _ in range(TOP_K):
        mx = jnp.max(cur, axis=0, keepdims=True)
        ix = jnp.min(jnp.where(cur == mx, iota, N_EXPERTS), axis=0, keepdims=True)
        vals.append(mx)
        idxs.append(ix)
        cur = jnp.where(iota == ix, -jnp.inf, cur)
    tv = jnp.concatenate(vals, axis=0)
    e = jnp.exp(tv - tv[0:1])
    tw_ref[0, 0] = e / jnp.sum(e, axis=0, keepdims=True)
    tidx_ref[0, 0] = jnp.concatenate(idxs, axis=0)


def _merge(xs, modtab, oa, ob, oc, gates, woa_t, wob_t, woc_t, wout, gffn, rw_t, rb_col, n_ctx_tiles):
    bsz, ntok, d = xs.shape
    nt = ntok // TM
    const = lambda b, i: (0, 0)
    tile4 = lambda b, i: (b, i, 0, 0)
    tok3 = lambda b, i: (b, i, 0)
    return pl.pallas_call(
        _merge_kernel,
        grid=(bsz, nt),
        in_specs=[pl.BlockSpec((1, TM, d), tok3),
                  pl.BlockSpec((1, 1, N_MOD, d), lambda b, i: (b, jnp.where(i < n_ctx_tiles, 0, 1), 0, 0)),
                  pl.BlockSpec((1, 1, ATTN_WIDTH, TM), tile4),
                  pl.BlockSpec((1, 1, ATTN_WIDTH, TM), tile4),
                  pl.BlockSpec((1, 1, ATTN_WIDTH, TM), tile4),
                  pl.BlockSpec((1, 1, 3 * d, TM), tile4),
                  pl.BlockSpec(woa_t.shape, const),
                  pl.BlockSpec(wob_t.shape, const),
                  pl.BlockSpec(woc_t.shape, const),
                  pl.BlockSpec(wout.shape, const),
                  pl.BlockSpec((1, d), const),
                  pl.BlockSpec(rw_t.shape, const),
                  pl.BlockSpec(rb_col.shape, const)],
        out_specs=[pl.BlockSpec((1, TM, d), tok3),
                   pl.BlockSpec((1, TM, d), tok3),
                   pl.BlockSpec((1, 1, TOP_K, TM), tile4),
                   pl.BlockSpec((1, 1, TOP_K, TM), tile4)],
        out_shape=[jax.ShapeDtypeStruct((bsz, ntok, d), F32),
                   jax.ShapeDtypeStruct((bsz, ntok, d), F32),
                   jax.ShapeDtypeStruct((bsz, nt, TOP_K, TM), jnp.int32),
                   jax.ShapeDtypeStruct((bsz, nt, TOP_K, TM), F32)],
        compiler_params=_cparams(("arbitrary", "arbitrary")),
        name="merge_router",
    )(xs, modtab, oa, ob, oc, gates, woa_t, wob_t, woc_t, wout, gffn, rw_t, rb_col)


def _moe_kernel(ib_ref, ie_ref, lo_ref, hi_ref, first_ref, nused_ref,
                src_cur_ref, dst_cur_ref, src_next_ref,
                x_hbm, wgu_ref, bgu_ref, wdn_ref, bdn_ref, y_hbm,
                xbuf0, xbuf1, ybuf0, ybuf1, wgu_bf, wdn_bf, gsem, ssem, *, spare_row0):
    i = pl.program_id(0)
    n_used = nused_ref[0]
    d = D_MODEL

    def gather_row(src_ref, r, xb, sem):
        return pltpu.make_async_copy(x_hbm.at[pl.ds(src_ref[0, 0, r], 1), :], xb.at[pl.ds(r, 1), :], sem)

    def scatter_row(r, lo, hi, yb, sem, spare):
        row = jnp.where((lo <= r) & (r < hi), dst_cur_ref[0, 0, r], spare + r)
        return pltpu.make_async_copy(yb.at[pl.ds(r, 1), :], y_hbm.at[pl.ds(row, 1), :], sem)

    def wait_gather(xb, sem):
        pltpu.make_async_copy(x_hbm.at[pl.ds(0, MOE_TB), :], xb, sem).wait()

    def wait_scatter(yb, sem):
        pltpu.make_async_copy(yb, y_hbm.at[pl.ds(0, MOE_TB), :], sem).wait()

    def step(p):
        xb_cur, xb_nxt = (xbuf0, xbuf1) if p == 0 else (xbuf1, xbuf0)
        yb_cur, yb_prv = (ybuf0, ybuf1) if p == 0 else (ybuf1, ybuf0)
        q = 1 - p

        if p == 0:
            @pl.when(i == 0)
            def _():
                for r in range(MOE_TB):
                    gather_row(src_cur_ref, r, xb_cur, gsem.at[p]).start(priority=r % 2)
                yb_prv[...] = jnp.zeros_like(yb_prv)
                for half in range(2):
                    spare = pltpu.make_async_copy(
                        yb_prv, y_hbm.at[pl.ds(spare_row0 + half * MOE_TB, MOE_TB), :], ssem.at[q])
                    spare.start()
                    spare.wait()

        @pl.when(first_ref[i] == 1)
        def _():
            wgu_bf[...] = wgu_ref[0, 0].astype(BF16)
            wdn_bf[...] = wdn_ref[0, 0].astype(BF16)

        for r in range(MOE_TB):
            gather_row(src_next_ref, r, xb_nxt, gsem.at[q]).start(priority=r % 2)
        wait_gather(xb_cur, gsem.at[p])

        xb = xb_cur[...].astype(BF16)
        y = None
        for c in range(d // MOE_CW):
            c0, c1 = c * MOE_CW, (c + 1) * MOE_CW
            glu = jnp.dot(xb, wgu_bf[:, c0:c1], preferred_element_type=F32) + bgu_ref[0, 0, :, c0:c1]
            lin = jnp.dot(xb, wgu_bf[:, d + c0:d + c1], preferred_element_type=F32) + bgu_ref[0, 0, :, d + c0:d + c1]
            glu = jnp.minimum(glu, SWIGLU_LIMIT)
            lin = jnp.clip(lin, -SWIGLU_LIMIT, SWIGLU_LIMIT)
            act = glu * jax.nn.sigmoid(SWIGLU_ALPHA * glu) * (lin + 1.0)
            part = jnp.dot(act.astype(BF16), wdn_bf[c0:c1, :], preferred_element_type=F32)
            y = part if y is None else y + part
        y = y + bdn_ref[0, 0]

        @pl.when(i >= 2)
        def _():
            wait_scatter(yb_cur, ssem.at[p])

        yb_cur[...] = y
        lo_cur, hi_cur = lo_ref[i], hi_ref[i]
        for r in range(MOE_TB):
            scatter_row(r, lo_cur, hi_cur, yb_cur, ssem.at[p], spare_row0 + p * MOE_TB).start(priority=r % 2)

        @pl.when(i == n_used - 1)
        def _():
            @pl.when(i >= 1)
            def _():
                wait_scatter(yb_prv, ssem.at[q])
            wait_scatter(yb_cur, ssem.at[p])
            wait_gather(xb_nxt, gsem.at[q])

    for p in range(2):
        @pl.when((i < n_used) & (i % 2 == p))
        def _(p=p):
            step(p)


def _moe_experts(h2_flat, plan, w_gu, b_gu, w_dn, b_dn, layer):
    item_b, item_e, lo, hi, first, n_used, src_sorted, dst_sorted = plan
    n_items = item_b.shape[0]
    n_tok, d = h2_flat.shape
    nb = src_sorted.shape[0] // MOE_TB

    def idx_block(f):
        return pl.BlockSpec((1, 1, MOE_TB), lambda i, ib, ie, lo_, hi_, fs, nu: (ib[f(i)], 0, 0),
                            memory_space=pltpu.SMEM)
    wmap = lambda i, ib, ie, lo_, hi_, fs, nu: (layer, ie[i], 0, 0)
    grid_spec = pltpu.PrefetchScalarGridSpec(
        num_scalar_prefetch=6,
        grid=(n_items,),
        in_specs=[idx_block(lambda i: i),
                  idx_block(lambda i: i),
                  idx_block(lambda i: jnp.minimum(i + 1, n_items - 1)),
                  pl.BlockSpec(memory_space=pl.ANY),
                  pl.BlockSpec((1, 1, d, 2 * d), wmap),
                  pl.BlockSpec((1, 1, 1, 2 * d), wmap),
                  pl.BlockSpec((1, 1, d, d), wmap),
                  pl.BlockSpec((1, 1, 1, d), wmap)],
        out_specs=pl.BlockSpec(memory_space=pl.ANY),
        scratch_shapes=[pltpu.VMEM((MOE_TB, d), F32), pltpu.VMEM((MOE_TB, d), F32),
                        pltpu.VMEM((MOE_TB, d), F32), pltpu.VMEM((MOE_TB, d), F32),
                        pltpu.VMEM((d, 2 * d), BF16), pltpu.VMEM((d, d), BF16),
                        pltpu.SemaphoreType.DMA((2,)), pltpu.SemaphoreType.DMA((2,))],
    )
    depth, ne = b_gu.shape[0], b_gu.shape[1]
    src3 = src_sorted.reshape(nb, 1, MOE_TB)
    dst3 = dst_sorted.reshape(nb, 1, MOE_TB)
    return pl.pallas_call(
        functools.partial(_moe_kernel, spare_row0=TOP_K * n_tok),
        grid_spec=grid_spec,
        out_shape=jax.ShapeDtypeStruct((TOP_K * n_tok + 2 * MOE_TB, d), F32),
        compiler_params=pltpu.CompilerParams(dimension_semantics=("arbitrary",), vmem_limit_bytes=VMEM_LIMIT,
                                             has_side_effects=True),
        name="moe_experts",
    )(item_b, item_e, lo, hi, first, n_used, src3, dst3, src3, h2_flat,
      w_gu, b_gu.reshape(depth, ne, 1, 2 * d), w_dn, b_dn.reshape(depth, ne, 1, d))


def _moe_plan(top_idx):
    i32 = jnp.int32
    flat_e = top_idx.reshape(-1)
    n_assign = flat_e.shape[0]
    assert n_assign % MOE_TB == 0
    nb = n_assign // MOE_TB
    n_items = nb + N_EXPERTS
    order = jnp.argsort(flat_e, stable=True).astype(i32)
    counts = jnp.sum((flat_e[None, :] == jnp.arange(N_EXPERTS, dtype=i32)[:, None]).astype(i32), axis=1)
    ends = jnp.cumsum(counts)
    starts = ends - counts
    pos = jnp.sort(jnp.concatenate([jnp.arange(nb, dtype=i32) * MOE_TB, starts]))
    length = jnp.concatenate([pos[1:], jnp.full((1,), n_assign, i32)]) - pos
    keep = length > 0
    perm = jnp.argsort(jnp.where(keep, pos, n_assign + jnp.arange(n_items, dtype=i32)))
    pos, length, keep = pos[perm], length[perm], keep[perm]
    item_b = jnp.minimum(pos // MOE_TB, nb - 1)
    item_e = jnp.minimum(jnp.sum((pos[:, None] >= ends[None, :]).astype(i32), axis=1), N_EXPERTS - 1)
    lo = pos - item_b * MOE_TB
    hi = lo + length
    prev_e = jnp.concatenate([jnp.full((1,), -1, i32), item_e[:-1]])
    first = (keep & (item_e != prev_e)).astype(i32)
    n_used = jnp.sum(keep.astype(i32)).reshape(1)
    n_tok = n_assign // TOP_K
    k = lax.shift_right_logical(order, _TM_SHIFT) & (TOP_K - 1)
    tok = lax.shift_left(lax.shift_right_logical(order, _TM_SHIFT + _K_SHIFT), _TM_SHIFT) | (order & (TM - 1))
    return (item_b.astype(i32), item_e.astype(i32), lo.astype(i32), hi.astype(i32), first, n_used,
            tok.astype(i32), (k * n_tok + tok).astype(i32))


def _combine_kernel(x_ref, mod_ref, tw_ref, y0_ref, y1_ref, y2_ref, y3_ref, o_ref):
    tm = x_ref.shape[1]
    w8 = jnp.concatenate([tw_ref[0, 0], jnp.zeros((8 - TOP_K, tm), F32)], axis=0)
    wt = w8.T
    tot = (wt[:, 0:1] * y0_ref[...] + wt[:, 1:2] * y1_ref[...]) + (wt[:, 2:3] * y2_ref[...] + wt[:, 3:4] * y3_ref[...])
    o_ref[0] = x_ref[0] + mod_ref[0, 0, 5:6, :] * tot


def _combine(xs, modtab, top_w, y, n_ctx_tiles, skip_tiles):
    bsz, ntok, d = xs.shape
    nt = ntok // TM
    n_out = nt - skip_tiles
    yrow = lambda k: pl.BlockSpec((TM, d), lambda b, i: ((k * bsz + b) * nt + i + skip_tiles, 0))
    return pl.pallas_call(
        _combine_kernel,
        grid=(bsz, n_out),
        in_specs=[pl.BlockSpec((1, TM, d), lambda b, i: (b, i + skip_tiles, 0)),
                  pl.BlockSpec((1, 1, N_MOD, d),
                               lambda b, i: (b, jnp.where(i + skip_tiles < n_ctx_tiles, 0, 1), 0, 0)),
                  pl.BlockSpec((1, 1, TOP_K, TM), lambda b, i: (b, i + skip_tiles, 0, 0)),
                  yrow(0), yrow(1), yrow(2), yrow(3)],
        out_specs=pl.BlockSpec((1, TM, d), lambda b, i: (b, i, 0)),
        out_shape=jax.ShapeDtypeStruct((bsz, n_out * TM, d), F32),
        compiler_params=_cparams(("arbitrary", "arbitrary")),
        name="moe_combine",
    )(xs, modtab, top_w, y, y, y, y)


def _rope_table(n_ctx, n_lat):
    n_rows = n_lat // GRID_W
    row = jnp.repeat(jnp.arange(n_rows, dtype=F32), GRID_W)
    col = jnp.tile(jnp.arange(GRID_W, dtype=F32), n_rows)

    def table(rot_dim):
        axis_pairs = rot_dim // 4
        inv_freq = ROPE_THETA ** (-jnp.arange(axis_pairs, dtype=F32) / axis_pairs)
        ang = jnp.concatenate([row[:, None] * inv_freq, col[:, None] * inv_freq], axis=-1)
        cos = jnp.concatenate([jnp.ones((n_ctx, 2 * axis_pairs), F32), jnp.cos(ang)], axis=0)
        sin = jnp.concatenate([jnp.zeros((n_ctx, 2 * axis_pairs), F32), jnp.sin(ang)], axis=0)
        return cos, sin

    c64, s64 = table(A_DIM)
    c32, s32 = table(B_ROPE)
    tab = jnp.concatenate([c64, s64, c32, s32], axis=-1)
    ntok = n_ctx + n_lat
    return jnp.transpose(tab.reshape(ntok // TM, TM, tab.shape[1]), (0, 2, 1))


def _cols(v):
    return jnp.broadcast_to(v.astype(F32)[:, None], (v.shape[0], TM))


def kernel(x, c, ctx, c_ctx, w_mod, b_mod, norm_mix, norm_ffn, w_in, a_q_norm, a_k_norm, b_q_a_norm,
           b_kv_a_norm, b_w_uq, b_w_ukv, b_q_norm, b_k_norm, c_q_norm, c_k_norm, c_lambda, c_subln,
           w_o_a, w_o_b, w_o_c, w_out, router_w, router_b, exp_w_gu, exp_b_gu, exp_w_down, exp_b_down):
    bsz, n_lat, d = x.shape
    n_ctx = ctx.shape[1]
    depth = w_mod.shape[0]
    assert d == D_MODEL and n_ctx % TM == 0 and n_lat % TM == 0 and n_lat % GRID_W == 0 and bsz + 1 <= 8
    n_ctx_tiles = n_ctx // TM
    ntok = n_ctx + n_lat

    xs = jnp.concatenate([ctx, x], axis=1)
    cond8 = jnp.zeros((8, d), F32).at[:bsz].set(c).at[bsz].set(c_ctx)
    mod_all = _adaln(cond8, w_mod, b_mod).reshape(depth, 8, N_MOD, d)
    rope = _rope_table(n_ctx, n_lat)

    for l in range(depth):
        lam_init = 0.8 - 0.6 * math.exp(-0.3 * l)
        lq1, lk1, lq2, lk2 = c_lambda[l].astype(F32)
        lam = (jnp.exp(jnp.sum(lq1 * lk1)) - jnp.exp(jnp.sum(lq2 * lk2)) + lam_init).reshape(1)
        modtab = jnp.stack([jnp.broadcast_to(mod_all[l, bsz], (bsz, N_MOD, d)), mod_all[l, :bsz]], axis=1)
        gcol = jnp.concatenate([_cols(a_q_norm[l]), _cols(a_k_norm[l]), _cols(b_q_a_norm[l]),
                                _cols(b_kv_a_norm[l]), _cols(b_q_norm[l]), _cols(b_k_norm[l]),
                                _cols(c_q_norm[l]), _cols(c_k_norm[l])], axis=0)
        (qa, ka, va, qb, kb, vb, qc, kc, vc, gates) = _pre_mixer(
            xs, modtab, norm_mix[l].reshape(1, d), w_in[l].T.astype(BF16), b_w_uq[l].T.astype(BF16),
            b_w_ukv[l].T.astype(BF16), gcol, rope, n_ctx_tiles)

        gsub = _cols(c_subln[l])
        oa, ob = _attention(lam, gsub, [(qa, ka, va), (qb, kb, vb)],
                            ((_UNITS_A, False, 1.0), (_UNITS_B, False, 1.0)),
                            n_ctx_tiles=n_ctx_tiles, name="attn_gqa_mla")
        oc, = _attention(lam, gsub, [(qc, kc, vc)], ((_UNITS_C, True, 1.0 - lam_init),),
                         n_ctx_tiles=n_ctx_tiles, name="attn_diff")

        xs, h2, top_idx, top_w = _merge(
            xs, modtab, oa, ob, oc, gates, w_o_a[l].T.astype(BF16), w_o_b[l].T.astype(BF16),
            w_o_c[l].T.astype(BF16), w_out[l].astype(BF16), norm_ffn[l].reshape(1, d),
            router_w[l].T.astype(BF16), _cols(router_b[l]), n_ctx_tiles)

        y = _moe_experts(h2.reshape(bsz * ntok, d), _moe_plan(top_idx), exp_w_gu, exp_b_gu, exp_w_down,
                         exp_b_down, l)
        last = l == depth - 1
        xs = _combine(xs, modtab, top_w, y, n_ctx_tiles, n_ctx_tiles if last else 0)
    return xs
```
